```python
import jax, jax.numpy as jnp
from jax import lax
import numpy as np

D_MODEL = 1024
BATCH = 4
SEQ = 8192
DEPTH = 4

N_META = 16
N_HEADS = 16
HEAD_DIM = D_MODEL // N_HEADS
D_FF = 4 * D_MODEL
CONV_WIDTH = 3
BLOCK_Q = 128
N_A_LAYERS = DEPTH // 2
N_B_LAYERS = DEPTH - N_A_LAYERS
N_NORMS = 4
RMS_EPS = 1e-6
NORM_NOISE = 0.02

kernel_name = "yoco_shortconv_stickbreaking_hybrid"


def rms_norm(x, g):
    xf = x.astype(jnp.float32)
    y = xf * lax.rsqrt(jnp.mean(xf * xf, axis=-1, keepdims=True) + RMS_EPS)
    return (y * g.astype(jnp.float32)).astype(x.dtype)


def short_conv_mixer(h, w_in, conv_w, w_out):
    L = h.shape[1]
    b_gate, c_gate, xv = jnp.split(h @ w_in, 3, axis=-1)
    u = c_gate * xv
    u_pad = jnp.pad(u, ((0, 0), (CONV_WIDTH - 1, 0), (0, 0)))
    conv = u_pad[:, 0:L] * conv_w[0]
    for tap in range(1, CONV_WIDTH):
        conv = conv + u_pad[:, tap:tap + L] * conv_w[tap]
    return (b_gate * conv) @ w_out


def squared_relu_mlp(h, w1, w2):
    return jnp.square(jax.nn.relu(h @ w1)) @ w2


def stick_breaking_attention(q, k, v):
    L = q.shape[2]
    pad = (-N_META) % BLOCK_Q
    widths = ((0, 0), (0, 0), (pad, 0), (0, 0))
    qp, kp, vp = jnp.pad(q, widths), jnp.pad(k, widths), jnp.pad(v, widths)
    n_blocks = (L + pad) // BLOCK_Q
    scale = HEAD_DIM ** -0.5
    outs = []
    for i in range(n_blocks):
        q_lo = i * BLOCK_Q
        kv_hi = q_lo + BLOCK_Q
        qb = qp[:, :, q_lo:kv_hi]
        kb = kp[:, :, :kv_hi]
        vb = vp[:, :, :kv_hi]
        z = jnp.einsum('bhqd,bhkd->bhqk', qb, kb).astype(jnp.float32) * scale
        q_pos = q_lo + jnp.arange(BLOCK_Q)[:, None]
        k_pos = jnp.arange(kv_hi)[None, :]
        visible = (k_pos < q_pos) & (k_pos >= pad)
        log_beta = jax.nn.log_sigmoid(z)
        log_one_minus = jnp.where(visible, jax.nn.log_sigmoid(-z), 0.0)
        rev = lax.cumsum(log_one_minus, axis=3, reverse=True)
        after = jnp.pad(rev[..., 1:], ((0, 0), (0, 0), (0, 0), (0, 1)))
        a = jnp.where(visible, jnp.exp(log_beta + after), 0.0)
        outs.append(jnp.einsum('bhqk,bhkd->bhqd', a.astype(vb.dtype), vb))
    return jnp.concatenate(outs, axis=2)[:, :, pad:]


def setup_inputs(seed: int = 0) -> dict:
    key = jax.random.key(seed)
    ks = jax.random.split(key, 14)
    nA, nB = N_A_LAYERS, N_B_LAYERS
    f32 = jnp.float32
    dm = D_MODEL ** -0.5
    return {
        "x": jax.random.normal(ks[0], (BATCH, SEQ, D_MODEL), f32),
        "meta_tokens": jax.random.normal(ks[1], (N_META, D_MODEL), f32),
        "norm_gains": 1.0 + NORM_NOISE * jax.random.normal(ks[2], (DEPTH, N_NORMS, D_MODEL), f32),
        "conv_in_proj": jax.random.normal(ks[3], (nA, D_MODEL, 3 * D_MODEL), f32) * dm,
        "conv_w": jax.random.normal(ks[4], (nA, CONV_WIDTH, D_MODEL), f32) * CONV_WIDTH ** -0.5,
        "conv_out_proj": jax.random.normal(ks[5], (nA, D_MODEL, D_MODEL), f32) * dm,
        "kv_norm": 1.0 + NORM_NOISE * jax.random.normal(ks[6], (D_MODEL,), f32),
        "w_k": jax.random.normal(ks[7], (D_MODEL, D_MODEL), f32) * dm,
        "w_v": jax.random.normal(ks[8], (D_MODEL, D_MODEL), f32) * dm,
        "w_q": jax.random.normal(ks[9], (nB, D_MODEL, D_MODEL), f32) * dm,
        "w_o": jax.random.normal(ks[10], (nB, D_MODEL, D_MODEL), f32) * dm,
        "mlp_w1": jax.random.normal(ks[11], (DEPTH, D_MODEL, D_FF), f32) * dm,
        "mlp_w2": jax.random.normal(ks[12], (DEPTH, D_FF, D_MODEL), f32) * D_FF ** -0.5,
    }


def reference(x, meta_tokens, norm_gains, conv_in_proj, conv_w, conv_out_proj,
              kv_norm, w_k, w_v, w_q, w_o, mlp_w1, mlp_w2):
    bsz = x.shape[0]
    meta = jnp.broadcast_to(meta_tokens[None].astype(x.dtype), (bsz, N_META, D_MODEL))
    h = jnp.concatenate([meta, x], axis=1)
    L = h.shape[1]

    def split_heads(t):
        return t.reshape(bsz, L, N_HEADS, HEAD_DIM).transpose(0, 2, 1, 3)

    k_shared = v_shared = None
    for layer in range(DEPTH):
        g = norm_gains[layer]
        if layer < N_A_LAYERS:
            mix = short_conv_mixer(rms_norm(h, g[0]), conv_in_proj[layer], conv_w[layer],
                                   conv_out_proj[layer])
        else:
            if layer == N_A_LAYERS:
                hk = rms_norm(h, kv_norm)
                k_shared = split_heads(hk @ w_k)
                v_shared = split_heads(hk @ w_v)
            j = layer - N_A_LAYERS
            q = split_heads(rms_norm(h, g[0]) @ w_q[j])
            o = stick_breaking_attention(q, k_shared, v_shared)
            mix = o.transpose(0, 2, 1, 3).reshape(bsz, L, D_MODEL) @ w_o[j]
        h = h + rms_norm(mix, g[1])
        ff = squared_relu_mlp(rms_norm(h, g[2]), mlp_w1[layer], mlp_w2[layer])
        h = h + rms_norm(ff, g[3])
    return h[:, N_META:]
```

```python
import functools

import jax
import jax.numpy as jnp
from jax import lax
from jax.experimental import pallas as pl
from jax.experimental.pallas import tpu as pltpu

D_MODEL = 1024
N_META = 16
N_HEADS = 16
HEAD_DIM = D_MODEL // N_HEADS
D_FF = 4 * D_MODEL
CONV_WIDTH = 3
RMS_EPS = 1e-6

LANES = 128
SUBLANES = 8
HEADS_PER_TILE = LANES // HEAD_DIM
N_HEAD_TILES = D_MODEL // LANES

ROW_BLOCK = 512
Q_BLOCK = 256
K_BLOCK = 128
FF_CHUNK = 1024
VMEM_LIMIT_BYTES = 56 * 1024 * 1024
MASKED_SCORE = -1e30


def _rms_norm(x, g):
    ms = jnp.mean(x * x, axis=-1, keepdims=True)
    return (x * lax.rsqrt(ms + RMS_EPS)) * g


def _resident(shape):
    return pl.BlockSpec(shape, lambda *_: (0,) * len(shape), pipeline_mode=pl.Buffered(1))


def _params(semantics):
    return pltpu.CompilerParams(dimension_semantics=semantics,
                                vmem_limit_bytes=VMEM_LIMIT_BYTES)


def _conv_mixer_kernel(h_ref, g_ref, w_in_ref, cw_ref, w_out_ref, o_ref, u_ref):
    rows = h_ref.shape[0]

    @pl.when(pl.program_id(0) == 0)
    def _():
        u_ref[0:SUBLANES, :] = jnp.zeros((SUBLANES, D_MODEL), jnp.float32)

    h = h_ref[...]
    n = _rms_norm(h, g_ref[0:1, :]).astype(jnp.bfloat16)
    c_gate = jnp.dot(n, w_in_ref[:, D_MODEL:2 * D_MODEL], preferred_element_type=jnp.float32)
    xv = jnp.dot(n, w_in_ref[:, 2 * D_MODEL:], preferred_element_type=jnp.float32)
    u = c_gate * xv
    u_ref[SUBLANES:SUBLANES + rows, :] = u
    conv = (u_ref[SUBLANES - 2:SUBLANES - 2 + rows, :] * cw_ref[0:1, :]
            + u_ref[SUBLANES - 1:SUBLANES - 1 + rows, :] * cw_ref[1:2, :]
            + u * cw_ref[2:3, :])
    u_ref[0:SUBLANES, :] = u_ref[rows:rows + SUBLANES, :]
    b_gate = jnp.dot(n, w_in_ref[:, :D_MODEL], preferred_element_type=jnp.float32)
    mix = jnp.dot((b_gate * conv).astype(jnp.bfloat16), w_out_ref[...],
                  preferred_element_type=jnp.float32)
    o_ref[...] = h + _rms_norm(mix, g_ref[1:2, :])


def _conv_mixer(h, gains, w_in, conv_w, w_out):
    rows = h.shape[0]
    row_spec = pl.BlockSpec((ROW_BLOCK, D_MODEL), lambda i: (i, 0))
    return pl.pallas_call(
        _conv_mixer_kernel,
        out_shape=jax.ShapeDtypeStruct(h.shape, h.dtype),
        grid=(rows // ROW_BLOCK,),
        in_specs=[row_spec, _resident(gains.shape), _resident(w_in.shape),
                  _resident(conv_w.shape), _resident(w_out.shape)],
        out_specs=row_spec,
        scratch_shapes=[pltpu.VMEM((ROW_BLOCK + SUBLANES, D_MODEL), jnp.float32)],
        compiler_params=_params(("arbitrary",)),
        name="conv_mixer",
    )(h, gains, w_in, conv_w, w_out)


def _mlp_kernel(h_ref, g_ref, w1_ref, w2_ref, o_ref):
    h = h_ref[...]
    n = _rms_norm(h, g_ref[2:3, :]).astype(jnp.bfloat16)
    ff = jnp.zeros(h.shape, jnp.float32)
    for c in range(D_FF // FF_CHUNK):
        cols = slice(c * FF_CHUNK, (c + 1) * FF_CHUNK)
        a = jnp.maximum(jnp.dot(n, w1_ref[:, cols], preferred_element_type=jnp.float32), 0.0)
        ff = ff + jnp.dot((a * a).astype(jnp.bfloat16), w2_ref[cols, :],
                          preferred_element_type=jnp.float32)
    o_ref[...] = h + _rms_norm(ff, g_ref[3:4, :])


def _mlp(h, gains, w1, w2):
    rows = h.shape[0]
    row_spec = pl.BlockSpec((ROW_BLOCK, D_MODEL), lambda i: (i, 0))
    return pl.pallas_call(
        _mlp_kernel,
        out_shape=jax.ShapeDtypeStruct(h.shape, h.dtype),
        grid=(rows // ROW_BLOCK,),
        in_specs=[row_spec, _resident(gains.shape), _resident(w1.shape), _resident(w2.shape)],
        out_specs=row_spec,
        compiler_params=_params(("parallel",)),
        name="mlp",
    )(h, gains, w1, w2)


def _norm_proj_kernel(h_ref, g_ref, w_ref, o_ref):
    n = _rms_norm(h_ref[...], g_ref[...]).astype(jnp.bfloat16)
    o_ref[...] = jnp.dot(n, w_ref[...], preferred_element_type=jnp.float32).astype(o_ref.dtype)


def _norm_proj(h, gain, w):
    rows = h.shape[0]
    n_out = w.shape[1]
    return pl.pallas_call(
        _norm_proj_kernel,
        out_shape=jax.ShapeDtypeStruct((rows, n_out), jnp.bfloat16),
        grid=(rows // ROW_BLOCK,),
        in_specs=[pl.BlockSpec((ROW_BLOCK, D_MODEL), lambda i: (i, 0)),
                  _resident(gain.shape), _resident(w.shape)],
        out_specs=pl.BlockSpec((ROW_BLOCK, n_out), lambda i: (i, 0)),
        compiler_params=_params(("parallel",)),
        name="norm_proj",
    )(h, gain, w)


def _attn_out_kernel(h_ref, o_ref_in, g_ref, w_ref, out_ref):
    mix = jnp.dot(o_ref_in[...], w_ref[...], preferred_element_type=jnp.float32)
    out_ref[...] = h_ref[...] + _rms_norm(mix, g_ref[1:2, :])


def _attn_out(h, o, gains, w_o):
    rows = h.shape[0]
    row_spec = pl.BlockSpec((ROW_BLOCK, D_MODEL), lambda i: (i, 0))
    return pl.pallas_call(
        _attn_out_kernel,
        out_shape=jax.ShapeDtypeStruct(h.shape, h.dtype),
        grid=(rows // ROW_BLOCK,),
        in_specs=[row_spec, row_spec, _resident(gains.shape), _resident(w_o.shape)],
        out_specs=row_spec,
        compiler_params=_params(("parallel",)),
        name="attn_out",
    )(h, o, gains, w_o)


def _attention_kernel(q_ref, k_ref, v_ref, scan_ref, o_ref):
    qi = pl.program_id(2)
    q = q_ref[0]
    lane = lax.broadcasted_iota(jnp.int32, (Q_BLOCK, LANES), 1)
    row = lax.broadcasted_iota(jnp.int32, (Q_BLOCK, K_BLOCK), 0)
    col = lax.broadcasted_iota(jnp.int32, (Q_BLOCK, K_BLOCK), 1)
    scan = scan_ref[...]

    def tile(q_head, k_tile, v_tile, run, acc, diag_offset):
        z = lax.dot_general(q_head, k_tile, (((1,), (1,)), ((), ())),
                            preferred_element_type=jnp.float32)
        if diag_offset is not None:
            z = jnp.where(col + diag_offset < row, z, MASKED_SCORE)
        sp = jnp.maximum(z, 0.0) + jnp.log(1.0 + jnp.exp(-jnp.abs(z)))
        hi = sp.astype(jnp.bfloat16)
        lo = (sp - hi.astype(jnp.float32)).astype(jnp.bfloat16)
        sums = jnp.dot(jnp.concatenate([hi, lo], axis=1), scan,
                       preferred_element_type=jnp.float32)
        a = jnp.exp(z - (run + sums[:, :K_BLOCK]))
        acc = acc + jnp.dot(a.astype(jnp.bfloat16), v_tile, preferred_element_type=jnp.float32)
        return run + sums[:, K_BLOCK:], acc

    tiles_per_q = Q_BLOCK // K_BLOCK
    outs = []
    for head in range(HEADS_PER_TILE):
        in_head = (lane // HEAD_DIM) == head
        q_head = jnp.where(in_head, q.astype(jnp.float32), 0.0).astype(jnp.bfloat16)
        run = jnp.zeros((Q_BLOCK, K_BLOCK), jnp.float32)
        acc = jnp.zeros((Q_BLOCK, LANES), jnp.float32)
        for d in reversed(range(tiles_per_q)):
            start = pl.multiple_of(qi * Q_BLOCK + d * K_BLOCK, K_BLOCK)
            run, acc = tile(q_head, k_ref[0, pl.ds(start, K_BLOCK), :],
                            v_ref[0, pl.ds(start, K_BLOCK), :], run, acc, d * K_BLOCK)

        def body(t, carry):
            run, acc = carry
            start = pl.multiple_of((qi * tiles_per_q - 1 - t) * K_BLOCK, K_BLOCK)
            return tile(q_head, k_ref[0, pl.ds(start, K_BLOCK), :],
                        v_ref[0, pl.ds(start, K_BLOCK), :], run, acc, None)

        run, acc = lax.fori_loop(0, qi * tiles_per_q, body, (run, acc))
        outs.append(acc)
    lane_o = lax.broadcasted_iota(jnp.int32, (Q_BLOCK, LANES), 1)
    o_ref[0] = jnp.where(lane_o < HEAD_DIM, outs[0], outs[1]).astype(o_ref.dtype)


def _scan_matrix():
    j = lax.broadcasted_iota(jnp.int32, (2 * K_BLOCK, 2 * K_BLOCK), 0) % K_BLOCK
    s = lax.broadcasted_iota(jnp.int32, (2 * K_BLOCK, 2 * K_BLOCK), 1)
    return ((s >= K_BLOCK) | (j >= s)).astype(jnp.bfloat16)


def _attention(q, kv, batch, l_pad):
    q3 = q.reshape(batch, l_pad, D_MODEL)
    kv3 = kv.reshape(batch, l_pad, 2 * D_MODEL)
    scan = _scan_matrix()
    out = pl.pallas_call(
        _attention_kernel,
        out_shape=jax.ShapeDtypeStruct(q3.shape, jnp.bfloat16),
        grid=(batch, N_HEAD_TILES, l_pad // Q_BLOCK),
        in_specs=[pl.BlockSpec((1, Q_BLOCK, LANES), lambda b, p, i: (b, i, p)),
                  pl.BlockSpec((1, l_pad, LANES), lambda b, p, i: (b, 0, p)),
                  pl.BlockSpec((1, l_pad, LANES), lambda b, p, i: (b, 0, N_HEAD_TILES + p)),
                  _resident(scan.shape)],
        out_specs=pl.BlockSpec((1, Q_BLOCK, LANES), lambda b, p, i: (b, i, p)),
        compiler_params=_params(("parallel", "parallel", "arbitrary")),
        name="stick_breaking_attention",
    )(q3, kv3, kv3, scan)
    return out.reshape(batch * l_pad, D_MODEL)


def kernel(x, meta_tokens, norm_gains, conv_in_proj, conv_w, conv_out_proj, kv_norm,
           w_k, w_v, w_q, w_o, mlp_w1, mlp_w2):
    batch, seq, _ = x.shape
    depth = norm_gains.shape[0]
    n_conv = conv_in_proj.shape[0]
    l_real = N_META + seq
    l_pad = -(-(l_real + CONV_WIDTH - 1) // Q_BLOCK) * Q_BLOCK
    while (batch * l_pad) % ROW_BLOCK:
        l_pad += Q_BLOCK
    bf16 = jnp.bfloat16

    meta = jnp.broadcast_to(meta_tokens[None].astype(x.dtype), (batch, N_META, D_MODEL))
    tail = jnp.zeros((batch, l_pad - l_real, D_MODEL), x.dtype)
    h = jnp.concatenate([meta, x, tail], axis=1).reshape(batch * l_pad, D_MODEL)

    kv = None
    for layer in range(depth):
        g = norm_gains[layer]
        if layer < n_conv:
            h = _conv_mixer(h, g, conv_in_proj[layer].astype(bf16), conv_w[layer],
                            conv_out_proj[layer].astype(bf16))
        else:
            if layer == n_conv:
                w_kv = jnp.concatenate([w_k, w_v], axis=1).astype(bf16)
                kv = _norm_proj(h, kv_norm[None, :], w_kv)
            j = layer - n_conv
            w_q_scaled = (w_q[j] * HEAD_DIM ** -0.5).astype(bf16)
            q = _norm_proj(h, g[0:1], w_q_scaled)
            o = _attention(q, kv, batch, l_pad)
            h = _attn_out(h, o, g, w_o[j].astype(bf16))
        h = _mlp(h, g, mlp_w1[layer].astype(bf16), mlp_w2[layer].astype(bf16))
    return h.reshape(batch, l_pad, D_MODEL)[:, N_META:l_real]
```

```python
import jax
import jax.numpy as jnp
from jax import lax
from jax.experimental import pallas as pl
from jax.experimental.pallas import tpu as pltpu

D_MODEL = 1024
N_META = 16
N_HEADS = 16
HEAD_DIM = D_MODEL // N_HEADS
D_FF = 4 * D_MODEL
CONV_WIDTH = 3
RMS_EPS = 1e-6

LANES = 128
SUBLANES = 8
HEADS_PER_TILE = LANES // HEAD_DIM
N_HEAD_TILES = D_MODEL // LANES

ROW_BLOCK = 512
K_BLOCK = 128
Q_BLOCK = 2 * K_BLOCK
FRONT_PAD = K_BLOCK
FF_CHUNK = 1024
VMEM_LIMIT_BYTES = 56 * 1024 * 1024
MASKED_SCORE = -1e30
STICK_GONE = 90.0


def _rms_norm(x, g):
    ms = jnp.mean(x * x, axis=-1, keepdims=True)
    return (x * lax.rsqrt(ms + RMS_EPS)) * g


def _resident(shape):
    return pl.BlockSpec(shape, lambda *_: (0,) * len(shape), pipeline_mode=pl.Buffered(1))


def _params(semantics):
    return pltpu.CompilerParams(dimension_semantics=semantics,
                                vmem_limit_bytes=VMEM_LIMIT_BYTES)


def _conv_mixer_kernel(h_ref, g_ref, w_in_ref, cw_ref, w_out_ref, o_ref, u_ref):
    rows = h_ref.shape[0]

    @pl.when(pl.program_id(0) == 0)
    def _():
        u_ref[0:SUBLANES, :] = jnp.zeros((SUBLANES, D_MODEL), jnp.float32)

    h = h_ref[...]
    n = _rms_norm(h, g_ref[0:1, :]).astype(jnp.bfloat16)
    c_gate = jnp.dot(n, w_in_ref[:, D_MODEL:2 * D_MODEL], preferred_element_type=jnp.float32)
    xv = jnp.dot(n, w_in_ref[:, 2 * D_MODEL:], preferred_element_type=jnp.float32)
    u = c_gate * xv
    u_ref[SUBLANES:SUBLANES + rows, :] = u
    conv = (u_ref[SUBLANES - 2:SUBLANES - 2 + rows, :] * cw_ref[0:1, :]
            + u_ref[SUBLANES - 1:SUBLANES - 1 + rows, :] * cw_ref[1:2, :]
            + u * cw_ref[2:3, :])
    u_ref[0:SUBLANES, :] = u_ref[rows:rows + SUBLANES, :]
    b_gate = jnp.dot(n, w_in_ref[:, :D_MODEL], preferred_element_type=jnp.float32)
    mix = jnp.dot((b_gate * conv).astype(jnp.bfloat16), w_out_ref[...],
                  preferred_element_type=jnp.float32)
    o_ref[...] = h + _rms_norm(mix, g_ref[1:2, :])


def _conv_mixer(h, gains, w_in, conv_w, w_out):
    rows = h.shape[0]
    row_spec = pl.BlockSpec((ROW_BLOCK, D_MODEL), lambda i: (i, 0))
    return pl.pallas_call(
        _conv_mixer_kernel,
        out_shape=jax.ShapeDtypeStruct(h.shape, h.dtype),
        grid=(rows // ROW_BLOCK,),
        in_specs=[row_spec, _resident(gains.shape), _resident(w_in.shape),
                  _resident(conv_w.shape), _resident(w_out.shape)],
        out_specs=row_spec,
        scratch_shapes=[pltpu.VMEM((ROW_BLOCK + SUBLANES, D_MODEL), jnp.float32)],
        compiler_params=_params(("arbitrary",)),
        name="conv_mixer",
    )(h, gains, w_in, conv_w, w_out)


def _mlp_kernel(h_ref, g_ref, w1_ref, w2_ref, o_ref):
    h = h_ref[...]
    n = _rms_norm(h, g_ref[2:3, :]).astype(jnp.bfloat16)
    ff = jnp.zeros(h.shape, jnp.float32)
    for c in range(D_FF // FF_CHUNK):
        cols = slice(c * FF_CHUNK, (c + 1) * FF_CHUNK)
        a = jnp.maximum(jnp.dot(n, w1_ref[:, cols], preferred_element_type=jnp.float32), 0.0)
        ff = ff + jnp.dot((a * a).astype(jnp.bfloat16), w2_ref[cols, :],
                          preferred_element_type=jnp.float32)
    o_ref[...] = h + _rms_norm(ff, g_ref[3:4, :])


def _mlp(h, gains, w1, w2):
    rows = h.shape[0]
    row_spec = pl.BlockSpec((ROW_BLOCK, D_MODEL), lambda i: (i, 0))
    return pl.pallas_call(
        _mlp_kernel,
        out_shape=jax.ShapeDtypeStruct(h.shape, h.dtype),
        grid=(rows // ROW_BLOCK,),
        in_specs=[row_spec, _resident(gains.shape), _resident(w1.shape), _resident(w2.shape)],
        out_specs=row_spec,
        compiler_params=_params(("parallel",)),
        name="mlp",
    )(h, gains, w1, w2)


def _norm_proj_kernel(h_ref, g_ref, w_ref, o_ref):
    n = _rms_norm(h_ref[...], g_ref[...]).astype(jnp.bfloat16)
    o_ref[...] = jnp.dot(n, w_ref[...], preferred_element_type=jnp.float32).astype(o_ref.dtype)


def _norm_proj(h, gain, w):
    rows = h.shape[0]
    n_out = w.shape[1]
    return pl.pallas_call(
        _norm_proj_kernel,
        out_shape=jax.ShapeDtypeStruct((rows, n_out), jnp.bfloat16),
        grid=(rows // ROW_BLOCK,),
        in_specs=[pl.BlockSpec((ROW_BLOCK, D_MODEL), lambda i: (i, 0)),
                  _resident(gain.shape), _resident(w.shape)],
        out_specs=pl.BlockSpec((ROW_BLOCK, n_out), lambda i: (i, 0)),
        compiler_params=_params(("parallel",)),
        name="norm_proj",
    )(h, gain, w)


def _attn_out_kernel(h_ref, o_ref_in, g_ref, w_ref, out_ref):
    mix = jnp.dot(o_ref_in[...], w_ref[...], preferred_element_type=jnp.float32)
    out_ref[...] = h_ref[...] + _rms_norm(mix, g_ref[1:2, :])


def _attn_out(h, o, gains, w_o):
    rows = h.shape[0]
    row_spec = pl.BlockSpec((ROW_BLOCK, D_MODEL), lambda i: (i, 0))
    return pl.pallas_call(
        _attn_out_kernel,
        out_shape=jax.ShapeDtypeStruct(h.shape, h.dtype),
        grid=(rows // ROW_BLOCK,),
        in_specs=[row_spec, row_spec, _resident(gains.shape), _resident(w_o.shape)],
        out_specs=row_spec,
        compiler_params=_params(("parallel",)),
        name="attn_out",
    )(h, o, gains, w_o)


def _attention_kernel(q_ref, k_ref, v_ref, scan_ref, o_ref):
    qi = pl.program_id(2)
    n_sub = Q_BLOCK // K_BLOCK
    chain_rows = HEADS_PER_TILE * K_BLOCK
    lane = lax.broadcasted_iota(jnp.int32, (K_BLOCK, LANES), 1)
    row = lax.broadcasted_iota(jnp.int32, (chain_rows, K_BLOCK), 0) % K_BLOCK
    col = lax.broadcasted_iota(jnp.int32, (chain_rows, K_BLOCK), 1)
    causal = col < row
    scan = scan_ref[...]

    q_stack = []
    for r in range(n_sub):
        q_sub = q_ref[0, r * K_BLOCK:(r + 1) * K_BLOCK, :].astype(jnp.float32)
        q_stack.append(jnp.concatenate(
            [jnp.where(lane // HEAD_DIM == head, q_sub, 0.0).astype(jnp.bfloat16)
             for head in range(HEADS_PER_TILE)], axis=0))

    def step(t, run, acc, masked):
        zs, v_tiles = [], []
        for r in range(n_sub):
            tile = jnp.maximum(qi * n_sub + r - t, 0)
            start = pl.multiple_of(tile * K_BLOCK, K_BLOCK)
            k_tile = k_ref[0, pl.ds(start, K_BLOCK), :]
            v_tiles.append(v_ref[0, pl.ds(start, K_BLOCK), :])
            z = lax.dot_general(q_stack[r], k_tile, (((1,), (1,)), ((), ())),
                                preferred_element_type=jnp.float32)
            zs.append(jnp.where(causal, z, MASKED_SCORE) if masked else z)
        z = jnp.concatenate(zs, axis=0)
        sp = jnp.maximum(z, 0.0) + jnp.log(1.0 + jnp.exp(-jnp.abs(z)))
        hi = sp.astype(jnp.bfloat16)
        lo = (sp - hi.astype(jnp.float32)).astype(jnp.bfloat16)
        sums = jnp.dot(jnp.concatenate([hi, lo], axis=1), scan,
                       preferred_element_type=jnp.float32)
        a = jnp.exp(z - (run + sums[:, :K_BLOCK])).astype(jnp.bfloat16)
        pv = [jnp.dot(a[r * chain_rows:(r + 1) * chain_rows], v_tiles[r],
                      preferred_element_type=jnp.float32) for r in range(n_sub)]
        return run + sums[:, K_BLOCK:], acc + jnp.concatenate(pv, axis=0)

    zeros = jnp.zeros((n_sub * chain_rows, K_BLOCK), jnp.float32)
    run, acc = step(0, zeros, zeros, True)
    last_step = qi * n_sub + n_sub - 1

    def cond(carry):
        t, gone, _, _ = carry
        return (t <= last_step) & (gone == 0)

    def body(carry):
        t, _, run, acc = carry
        run, acc = step(t, run, acc, False)
        return t + 1, (jnp.min(run) > STICK_GONE).astype(jnp.int32), run, acc

    _, _, _, acc = lax.while_loop(
        cond, body, (jnp.int32(1), (jnp.min(run) > STICK_GONE).astype(jnp.int32), run, acc))

    for r in range(n_sub):
        base = r * chain_rows
        o_ref[0, r * K_BLOCK:(r + 1) * K_BLOCK, :] = jnp.where(
            lane < HEAD_DIM, acc[base:base + K_BLOCK],
            acc[base + K_BLOCK:base + 2 * K_BLOCK]).astype(o_ref.dtype)


def _scan_matrix():
    j = lax.broadcasted_iota(jnp.int32, (2 * K_BLOCK, 2 * K_BLOCK), 0) % K_BLOCK
    s = lax.broadcasted_iota(jnp.int32, (2 * K_BLOCK, 2 * K_BLOCK), 1)
    return ((s >= K_BLOCK) | (j >= s)).astype(jnp.bfloat16)


def _attention(q, kv, batch, l_pad):
    q3 = q.reshape(batch, l_pad, D_MODEL)
    kv3 = kv.reshape(batch, l_pad, 2 * D_MODEL)
    scan = _scan_matrix()
    out = pl.pallas_call(
        _attention_kernel,
        out_shape=jax.ShapeDtypeStruct(q3.shape, jnp.bfloat16),
        grid=(batch, N_HEAD_TILES, l_pad // Q_BLOCK),
        in_specs=[pl.BlockSpec((1, Q_BLOCK, LANES), lambda b, p, i: (b, i, p)),
                  pl.BlockSpec((1, l_pad, LANES), lambda b, p, i: (b, 0, p)),
                  pl.BlockSpec((1, l_pad, LANES), lambda b, p, i: (b, 0, N_HEAD_TILES + p)),
                  _resident(scan.shape)],
        out_specs=pl.BlockSpec((1, Q_BLOCK, LANES), lambda b, p, i: (b, i, p)),
        compiler_params=_params(("parallel", "parallel", "arbitrary")),
        name="stick_breaking_attention",
    )(q3, kv3, kv3, scan)
    return out.reshape(batch * l_pad, D_MODEL)


def kernel(x, meta_tokens, norm_gains, conv_in_proj, conv_w, conv_out_proj, kv_norm,
           w_k, w_v, w_q, w_o, mlp_w1, mlp_w2):
    batch, seq, _ = x.shape
    depth = norm_gains.shape[0]
    n_conv = conv_in_proj.shape[0]
    first = FRONT_PAD
    last = first + N_META + seq
    l_pad = -(-(last + CONV_WIDTH - 1) // Q_BLOCK) * Q_BLOCK
    while (batch * l_pad) % ROW_BLOCK:
        l_pad += Q_BLOCK
    bf16 = jnp.bfloat16

    meta = jnp.broadcast_to(meta_tokens[None].astype(x.dtype), (batch, N_META, D_MODEL))
    h = jnp.concatenate([jnp.zeros((batch, first, D_MODEL), x.dtype), meta, x,
                         jnp.zeros((batch, l_pad - last, D_MODEL), x.dtype)], axis=1)
    h = h.reshape(batch * l_pad, D_MODEL)

    kv = None
    for layer in range(depth):
        g = norm_gains[layer]
        if layer < n_conv:
            h = _conv_mixer(h, g, conv_in_proj[layer].astype(bf16), conv_w[layer],
                            conv_out_proj[layer].astype(bf16))
        else:
            if layer == n_conv:
                w_kv = jnp.concatenate([w_k, w_v], axis=1).astype(bf16)
                kv = _norm_proj(h, kv_norm[None, :], w_kv)
            j = layer - n_conv
            w_q_scaled = (w_q[j] * HEAD_DIM ** -0.5).astype(bf16)
            q = _norm_proj(h, g[0:1], w_q_scaled)
            o = _attention(q, kv, batch, l_pad)
            h = _attn_out(h, o, g, w_o[j].astype(bf16))
        h = _mlp(h, g, mlp_w1[layer].astype(bf16), mlp_w2[layer].astype(bf16))
    return h.reshape(batch, l_pad, D_MODEL)[:, first + N_META:last]
```

```python
import jax
import jax.numpy as jnp
from jax import lax
from jax.experimental import pallas as pl
from jax.experimental.pallas import tpu as pltpu

D_MODEL = 1024
N_META = 16
N_HEADS = 16
HEAD_DIM = D_MODEL // N_HEADS
D_FF = 4 * D_MODEL
CONV_WIDTH = 3
RMS_EPS = 1e-6

LANES = 128
SUBLANES = 8
HEADS_PER_TILE = LANES // HEAD_DIM
N_HEAD_TILES = D_MODEL // LANES

ROW_BLOCK = 512
K_BLOCK = 128
Q_BLOCK = 6 * K_BLOCK
FRONT_PAD = K_BLOCK
FF_CHUNK = 1024
VMEM_LIMIT_BYTES = 56 * 1024 * 1024
MASKED_SCORE = -1e30
STICK_GONE = 90.0
SOFTPLUS_LINEAR = 20.0


def _rms_norm(x, g):
    ms = jnp.mean(x * x, axis=-1, keepdims=True)
    return (x * lax.rsqrt(ms + RMS_EPS)) * g


def _resident(shape):
    return pl.BlockSpec(shape, lambda *_: (0,) * len(shape), pipeline_mode=pl.Buffered(1))


def _params(semantics):
    return pltpu.CompilerParams(dimension_semantics=semantics,
                                vmem_limit_bytes=VMEM_LIMIT_BYTES)


def _conv_mixer_kernel(h_ref, g_ref, w_in_ref, cw_ref, w_out_ref, o_ref, u_ref):
    rows = h_ref.shape[0]

    @pl.when(pl.program_id(0) == 0)
    def _():
        u_ref[0:SUBLANES, :] = jnp.zeros((SUBLANES, D_MODEL), jnp.float32)

    h = h_ref[...]
    n = _rms_norm(h, g_ref[0:1, :]).astype(jnp.bfloat16)
    c_gate = jnp.dot(n, w_in_ref[:, D_MODEL:2 * D_MODEL], preferred_element_type=jnp.float32)
    xv = jnp.dot(n, w_in_ref[:, 2 * D_MODEL:], preferred_element_type=jnp.float32)
    u = c_gate * xv
    u_ref[SUBLANES:SUBLANES + rows, :] = u
    conv = (u_ref[SUBLANES - 2:SUBLANES - 2 + rows, :] * cw_ref[0:1, :]
            + u_ref[SUBLANES - 1:SUBLANES - 1 + rows, :] * cw_ref[1:2, :]
            + u * cw_ref[2:3, :])
    u_ref[0:SUBLANES, :] = u_ref[rows:rows + SUBLANES, :]
    b_gate = jnp.dot(n, w_in_ref[:, :D_MODEL], preferred_element_type=jnp.float32)
    mix = jnp.dot((b_gate * conv).astype(jnp.bfloat16), w_out_ref[...],
                  preferred_element_type=jnp.float32)
    o_ref[...] = h + _rms_norm(mix, g_ref[1:2, :])


def _conv_mixer(h, gains, w_in, conv_w, w_out):
    rows = h.shape[0]
    row_spec = pl.BlockSpec((ROW_BLOCK, D_MODEL), lambda i: (i, 0))
    return pl.pallas_call(
        _conv_mixer_kernel,
        out_shape=jax.ShapeDtypeStruct(h.shape, h.dtype),
        grid=(rows // ROW_BLOCK,),
        in_specs=[row_spec, _resident(gains.shape), _resident(w_in.shape),
                  _resident(conv_w.shape), _resident(w_out.shape)],
        out_specs=row_spec,
        scratch_shapes=[pltpu.VMEM((ROW_BLOCK + SUBLANES, D_MODEL), jnp.float32)],
        compiler_params=_params(("arbitrary",)),
        name="conv_mixer",
    )(h, gains, w_in, conv_w, w_out)


def _mlp_kernel(h_ref, g_ref, w1_ref, w2_ref, o_ref):
    h = h_ref[...]
    n = _rms_norm(h, g_ref[2:3, :]).astype(jnp.bfloat16)
    ff = jnp.zeros(h.shape, jnp.float32)
    for c in range(D_FF // FF_CHUNK):
        cols = slice(c * FF_CHUNK, (c + 1) * FF_CHUNK)
        a = jnp.maximum(jnp.dot(n, w1_ref[:, cols], preferred_element_type=jnp.float32), 0.0)
        ff = ff + jnp.dot((a * a).astype(jnp.bfloat16), w2_ref[cols, :],
                          preferred_element_type=jnp.float32)
    o_ref[...] = h + _rms_norm(ff, g_ref[3:4, :])


def _mlp(h, gains, w1, w2):
    rows = h.shape[0]
    row_spec = pl.BlockSpec((ROW_BLOCK, D_MODEL), lambda i: (i, 0))
    return pl.pallas_call(
        _mlp_kernel,
        out_shape=jax.ShapeDtypeStruct(h.shape, h.dtype),
        grid=(rows // ROW_BLOCK,),
        in_specs=[row_spec, _resident(gains.shape), _resident(w1.shape), _resident(w2.shape)],
        out_specs=row_spec,
        compiler_params=_params(("parallel",)),
        name="mlp",
    )(h, gains, w1, w2)


def _norm_proj_kernel(h_ref, g_ref, w_ref, o_ref):
    n = _rms_norm(h_ref[...], g_ref[...]).astype(jnp.bfloat16)
    o_ref[...] = jnp.dot(n, w_ref[...], preferred_element_type=jnp.float32).astype(o_ref.dtype)


def _norm_proj(h, gain, w):
    rows = h.shape[0]
    n_out = w.shape[1]
    return pl.pallas_call(
        _norm_proj_kernel,
        out_shape=jax.ShapeDtypeStruct((rows, n_out), jnp.bfloat16),
        grid=(rows // ROW_BLOCK,),
        in_specs=[pl.BlockSpec((ROW_BLOCK, D_MODEL), lambda i: (i, 0)),
                  _resident(gain.shape), _resident(w.shape)],
        out_specs=pl.BlockSpec((ROW_BLOCK, n_out), lambda i: (i, 0)),
        compiler_params=_params(("parallel",)),
        name="norm_proj",
    )(h, gain, w)


def _attn_out_kernel(h_ref, o_ref_in, g_ref, w_ref, out_ref):
    mix = jnp.dot(o_ref_in[...], w_ref[...], preferred_element_type=jnp.float32)
    out_ref[...] = h_ref[...] + _rms_norm(mix, g_ref[1:2, :])


def _attn_out(h, o, gains, w_o):
    rows = h.shape[0]
    row_spec = pl.BlockSpec((ROW_BLOCK, D_MODEL), lambda i: (i, 0))
    return pl.pallas_call(
        _attn_out_kernel,
        out_shape=jax.ShapeDtypeStruct(h.shape, h.dtype),
        grid=(rows // ROW_BLOCK,),
        in_specs=[row_spec, row_spec, _resident(gains.shape), _resident(w_o.shape)],
        out_specs=row_spec,
        compiler_params=_params(("parallel",)),
        name="attn_out",
    )(h, o, gains, w_o)


def _attention_kernel(q_ref, k_ref, v_ref, tri_ref, scan_ref, o_ref):
    qi = pl.program_id(2)
    n_sub = Q_BLOCK // K_BLOCK
    window = 2 * K_BLOCK
    chain_rows = HEADS_PER_TILE * K_BLOCK
    lane = lax.broadcasted_iota(jnp.int32, (K_BLOCK, LANES), 1)
    row = lax.broadcasted_iota(jnp.int32, (chain_rows, K_BLOCK), 0) % K_BLOCK
    col = lax.broadcasted_iota(jnp.int32, (chain_rows, K_BLOCK), 1)
    causal = col < row

    q_stack = []
    for r in range(n_sub):
        q_sub = q_ref[0, r * K_BLOCK:(r + 1) * K_BLOCK, :].astype(jnp.float32)
        q_stack.append(jnp.concatenate(
            [jnp.where(lane // HEAD_DIM == head, q_sub, 0.0).astype(jnp.bfloat16)
             for head in range(HEADS_PER_TILE)], axis=0))

    def scores(q_rows, k_rows):
        return lax.dot_general(q_rows, k_rows, (((1,), (1,)), ((), ())),
                               preferred_element_type=jnp.float32)

    def softplus_parts(z):
        sp = jnp.where(z > SOFTPLUS_LINEAR, z, jnp.log(1.0 + jnp.exp(z)))
        hi = sp.astype(jnp.bfloat16)
        return hi, (sp - hi.astype(jnp.float32)).astype(jnp.bfloat16)

    tri = tri_ref[...]
    starts = [pl.multiple_of(jnp.maximum(qi * n_sub + r - 1, 0) * K_BLOCK, K_BLOCK)
              for r in range(n_sub)]
    zs, belows, accs, totals = [], [], [], []
    for r in range(n_sub):
        z = scores(q_stack[r], k_ref[0, pl.ds(starts[r], window), :])
        zs.append(jnp.concatenate(
            [z[:, :K_BLOCK], jnp.where(causal, z[:, K_BLOCK:], MASKED_SCORE)], axis=1))
    for r in range(n_sub):
        hi, lo = softplus_parts(zs[r])
        belows.append(jnp.dot(hi, tri, preferred_element_type=jnp.float32)
                      + jnp.dot(lo, tri, preferred_element_type=jnp.float32))
    for r in range(n_sub):
        a = jnp.exp(zs[r] - belows[r]).astype(jnp.bfloat16)
        accs.append(jnp.dot(a, v_ref[0, pl.ds(starts[r], window), :],
                            preferred_element_type=jnp.float32))
        totals.append(belows[r][:, 0:1])

    def write_out(r, acc_r):
        o_ref[0, r * K_BLOCK:(r + 1) * K_BLOCK, :] = jnp.where(
            lane < HEAD_DIM, acc_r[:K_BLOCK], acc_r[K_BLOCK:]).astype(o_ref.dtype)

    def walk_on(r):
        diag = qi * n_sub + r

        def cond(carry):
            t, gone, _, _ = carry
            return (t <= diag) & (gone == 0)

        def body(carry):
            t, _, run, acc = carry
            start = pl.multiple_of((diag - t) * K_BLOCK, K_BLOCK)
            z = scores(q_stack[r], k_ref[0, pl.ds(start, K_BLOCK), :])
            hi, lo = softplus_parts(z)
            sums = jnp.dot(jnp.concatenate([hi, lo], axis=1), scan_ref[...],
                           preferred_element_type=jnp.float32)
            a = jnp.exp(z - (run + sums[:, :K_BLOCK])).astype(jnp.bfloat16)
            acc = acc + jnp.dot(a, v_ref[0, pl.ds(start, K_BLOCK), :],
                                preferred_element_type=jnp.float32)
            run = run + sums[:, K_BLOCK:]
            return t + 1, (jnp.min(run) > STICK_GONE).astype(jnp.int32), run, acc

        run = jnp.broadcast_to(totals[r], (chain_rows, K_BLOCK))
        _, _, _, acc = lax.while_loop(cond, body, (jnp.int32(2), jnp.int32(0), run, accs[r]))
        write_out(r, acc)

    for r in range(n_sub):
        write_out(r, accs[r])
        pl.when(jnp.min(totals[r]) <= STICK_GONE)(lambda r=r: walk_on(r))


def _scan_matrices():
    j = lax.broadcasted_iota(jnp.int32, (2 * K_BLOCK, 2 * K_BLOCK), 0)
    s = lax.broadcasted_iota(jnp.int32, (2 * K_BLOCK, 2 * K_BLOCK), 1)
    tri = (j >= s).astype(jnp.bfloat16)
    scan = ((s >= K_BLOCK) | (j % K_BLOCK >= s)).astype(jnp.bfloat16)
    return tri, scan


def _attention(q, kv, batch, l_pad):
    q3 = q.reshape(batch, l_pad, D_MODEL)
    kv3 = kv.reshape(batch, l_pad, 2 * D_MODEL)
    tri, scan = _scan_matrices()
    out = pl.pallas_call(
        _attention_kernel,
        out_shape=jax.ShapeDtypeStruct(q3.shape, jnp.bfloat16),
        grid=(batch, N_HEAD_TILES, l_pad // Q_BLOCK),
        in_specs=[pl.BlockSpec((1, Q_BLOCK, LANES), lambda b, p, i: (b, i, p)),
                  pl.BlockSpec((1, l_pad, LANES), lambda b, p, i: (b, 0, p)),
                  pl.BlockSpec((1, l_pad, LANES), lambda b, p, i: (b, 0, N_HEAD_TILES + p)),
                  _resident(tri.shape), _resident(scan.shape)],
        out_specs=pl.BlockSpec((1, Q_BLOCK, LANES), lambda b, p, i: (b, i, p)),
        compiler_params=_params(("parallel", "parallel", "arbitrary")),
        name="stick_breaking_attention",
    )(q3, kv3, kv3, tri, scan)
    return out.reshape(batch * l_pad, D_MODEL)


def kernel(x, meta_tokens, norm_gains, conv_in_proj, conv_w, conv_out_proj, kv_norm,
           w_k, w_v, w_q, w_o, mlp_w1, mlp_w2):
    batch, seq, _ = x.shape
    depth = norm_gains.shape[0]
    n_conv = conv_in_proj.shape[0]
    first = FRONT_PAD
    last = first + N_META + seq
    l_pad = -(-(last + CONV_WIDTH - 1) // Q_BLOCK) * Q_BLOCK
    while (batch * l_pad) % ROW_BLOCK:
        l_pad += Q_BLOCK
    bf16 = jnp.bfloat16

    meta = jnp.broadcast_to(meta_tokens[None].astype(x.dtype), (batch, N_META, D_MODEL))
    h = jnp.concatenate([jnp.zeros((batch, first, D_MODEL), x.dtype), meta, x,
                         jnp.zeros((batch, l_pad - last, D_MODEL), x.dtype)], axis=1)
    h = h.reshape(batch * l_pad, D_MODEL)

    kv = None
    for layer in range(depth):
        g = norm_gains[layer]
        if layer < n_conv:
            h = _conv_mixer(h, g, conv_in_proj[layer].astype(bf16), conv_w[layer],
                            conv_out_proj[layer].astype(bf16))
        else:
            if layer == n_conv:
                w_kv = jnp.concatenate([w_k, w_v], axis=1).astype(bf16)
                kv = _norm_proj(h, kv_norm[None, :], w_kv)
            j = layer - n_conv
            w_q_scaled = (w_q[j] * HEAD_DIM ** -0.5).astype(bf16)
            q = _norm_proj(h, g[0:1], w_q_scaled)
            o = _attention(q, kv, batch, l_pad)
            h = _attn_out(h, o, g, w_o[j].astype(bf16))
        h = _mlp(h, g, mlp_w1[layer].astype(bf16), mlp_w2[layer].astype(bf16))
    return h.reshape(batch, l_pad, D_MODEL)[:, first + N_META:last]
```

```python
import jax
import jax.numpy as jnp
from jax import lax
from jax.experimental import pallas as pl
from jax.experimental.pallas import tpu as pltpu

D_MODEL = 1024
N_META = 16
N_HEADS = 16
HEAD_DIM = D_MODEL // N_HEADS
D_FF = 4 * D_MODEL
CONV_WIDTH = 3
RMS_EPS = 1e-6

LANES = 128
SUBLANES = 8
HEADS_PER_TILE = LANES // HEAD_DIM
N_HEAD_TILES = D_MODEL // LANES

ROW_BLOCK = 512
K_BLOCK = 128
Q_BLOCK = 6 * K_BLOCK
FRONT_PAD = K_BLOCK
FF_CHUNK = 1024
VMEM_LIMIT_BYTES = 56 * 1024 * 1024
MASKED_SCORE = -1e30
STICK_GONE = 90.0
SOFTPLUS_LINEAR = 20.0


def _rms_norm(x, g):
    ms = jnp.mean(x * x, axis=-1, keepdims=True)
    return (x * lax.rsqrt(ms + RMS_EPS)) * g


def _resident(shape):
    return pl.BlockSpec(shape, lambda *_: (0,) * len(shape), pipeline_mode=pl.Buffered(1))


def _params(semantics):
    return pltpu.CompilerParams(dimension_semantics=semantics,
                                vmem_limit_bytes=VMEM_LIMIT_BYTES)


def _conv_mixer_kernel(h_ref, g_ref, w_in_ref, cw_ref, w_out_ref, o_ref, u_ref):
    rows = h_ref.shape[0]

    @pl.when(pl.program_id(0) == 0)
    def _():
        u_ref[0:SUBLANES, :] = jnp.zeros((SUBLANES, D_MODEL), jnp.float32)

    h = h_ref[...]
    n = _rms_norm(h, g_ref[0:1, :]).astype(jnp.bfloat16)
    c_gate = jnp.dot(n, w_in_ref[:, D_MODEL:2 * D_MODEL], preferred_element_type=jnp.float32)
    xv = jnp.dot(n, w_in_ref[:, 2 * D_MODEL:], preferred_element_type=jnp.float32)
    u = c_gate * xv
    u_ref[SUBLANES:SUBLANES + rows, :] = u
    conv = (u_ref[SUBLANES - 2:SUBLANES - 2 + rows, :] * cw_ref[0:1, :]
            + u_ref[SUBLANES - 1:SUBLANES - 1 + rows, :] * cw_ref[1:2, :]
            + u * cw_ref[2:3, :])
    u_ref[0:SUBLANES, :] = u_ref[rows:rows + SUBLANES, :]
    b_gate = jnp.dot(n, w_in_ref[:, :D_MODEL], preferred_element_type=jnp.float32)
    mix = jnp.dot((b_gate * conv).astype(jnp.bfloat16), w_out_ref[...],
                  preferred_element_type=jnp.float32)
    o_ref[...] = h + _rms_norm(mix, g_ref[1:2, :])


def _conv_mixer(h, gains, w_in, conv_w, w_out):
    rows = h.shape[0]
    row_spec = pl.BlockSpec((ROW_BLOCK, D_MODEL), lambda i: (i, 0))
    return pl.pallas_call(
        _conv_mixer_kernel,
        out_shape=jax.ShapeDtypeStruct(h.shape, h.dtype),
        grid=(rows // ROW_BLOCK,),
        in_specs=[row_spec, _resident(gains.shape), _resident(w_in.shape),
                  _resident(conv_w.shape), _resident(w_out.shape)],
        out_specs=row_spec,
        scratch_shapes=[pltpu.VMEM((ROW_BLOCK + SUBLANES, D_MODEL), jnp.float32)],
        compiler_params=_params(("arbitrary",)),
        name="conv_mixer",
    )(h, gains, w_in, conv_w, w_out)


def _mlp_residual(h, g_ref, w1_ref, w2_ref):
    n = _rms_norm(h, g_ref[2:3, :]).astype(jnp.bfloat16)
    ff = jnp.zeros(h.shape, jnp.float32)
    for c in range(D_FF // FF_CHUNK):
        cols = slice(c * FF_CHUNK, (c + 1) * FF_CHUNK)
        a = jnp.maximum(jnp.dot(n, w1_ref[:, cols], preferred_element_type=jnp.float32), 0.0)
        ff = ff + jnp.dot((a * a).astype(jnp.bfloat16), w2_ref[cols, :],
                          preferred_element_type=jnp.float32)
    return h + _rms_norm(ff, g_ref[3:4, :])


def _mlp_kernel(h_ref, g_ref, w1_ref, w2_ref, out_ref):
    out_ref[...] = _mlp_residual(h_ref[...], g_ref, w1_ref, w2_ref)


def _mlp(h, gains, w1, w2):
    rows = h.shape[0]
    row_spec = pl.BlockSpec((ROW_BLOCK, D_MODEL), lambda i: (i, 0))
    return pl.pallas_call(
        _mlp_kernel,
        out_shape=jax.ShapeDtypeStruct(h.shape, h.dtype),
        grid=(rows // ROW_BLOCK,),
        in_specs=[row_spec, _resident(gains.shape), _resident(w1.shape), _resident(w2.shape)],
        out_specs=row_spec,
        compiler_params=_params(("parallel",)),
        name="mlp",
    )(h, gains, w1, w2)


def _attn_mlp_kernel(h_ref, o_ref, g_ref, wo_ref, w1_ref, w2_ref, out_ref):
    o = jnp.concatenate([o_ref[c] for c in range(N_HEAD_TILES)], axis=1)
    mix = jnp.dot(o, wo_ref[...], preferred_element_type=jnp.float32)
    h = h_ref[...] + _rms_norm(mix, g_ref[1:2, :])
    out_ref[...] = _mlp_residual(h, g_ref, w1_ref, w2_ref)


def _attn_mlp(h, o, gains, w_o, w1, w2):
    rows = h.shape[0]
    row_spec = pl.BlockSpec((ROW_BLOCK, D_MODEL), lambda i: (i, 0))
    return pl.pallas_call(
        _attn_mlp_kernel,
        out_shape=jax.ShapeDtypeStruct(h.shape, h.dtype),
        grid=(rows // ROW_BLOCK,),
        in_specs=[row_spec, pl.BlockSpec((N_HEAD_TILES, ROW_BLOCK, LANES), lambda i: (0, i, 0)),
                  _resident(gains.shape), _resident(w_o.shape), _resident(w1.shape),
                  _resident(w2.shape)],
        out_specs=row_spec,
        compiler_params=_params(("parallel",)),
        name="attn_mlp",
    )(h, o, gains, w_o, w1, w2)


def _norm_proj_kernel(h_ref, g_ref, w_ref, o_ref):
    n = _rms_norm(h_ref[...], g_ref[...]).astype(jnp.bfloat16)
    y = jnp.dot(n, w_ref[...], preferred_element_type=jnp.float32).astype(o_ref.dtype)
    for c in range(o_ref.shape[0]):
        o_ref[c] = y[:, c * LANES:(c + 1) * LANES]


def _norm_proj(h, gain, w):
    rows = h.shape[0]
    n_tiles = w.shape[1] // LANES
    return pl.pallas_call(
        _norm_proj_kernel,
        out_shape=jax.ShapeDtypeStruct((n_tiles, rows, LANES), jnp.bfloat16),
        grid=(rows // ROW_BLOCK,),
        in_specs=[pl.BlockSpec((ROW_BLOCK, D_MODEL), lambda i: (i, 0)),
                  _resident(gain.shape), _resident(w.shape)],
        out_specs=pl.BlockSpec((n_tiles, ROW_BLOCK, LANES), lambda i: (0, i, 0)),
        compiler_params=_params(("parallel",)),
        name="norm_proj",
    )(h, gain, w)


def _attention_kernel(q_ref, k_ref, v_ref, tri_ref, scan_ref, o_ref):
    qi = pl.program_id(2)
    n_sub = Q_BLOCK // K_BLOCK
    window = 2 * K_BLOCK
    chain_rows = HEADS_PER_TILE * K_BLOCK
    lane = lax.broadcasted_iota(jnp.int32, (K_BLOCK, LANES), 1)
    row = lax.broadcasted_iota(jnp.int32, (chain_rows, K_BLOCK), 0) % K_BLOCK
    col = lax.broadcasted_iota(jnp.int32, (chain_rows, K_BLOCK), 1)
    causal = col < row

    q_stack = []
    for r in range(n_sub):
        q_sub = q_ref[r * K_BLOCK:(r + 1) * K_BLOCK, :].astype(jnp.float32)
        q_stack.append(jnp.concatenate(
            [jnp.where(lane // HEAD_DIM == head, q_sub, 0.0).astype(jnp.bfloat16)
             for head in range(HEADS_PER_TILE)], axis=0))

    def scores(q_rows, k_rows):
        return lax.dot_general(q_rows, k_rows, (((1,), (1,)), ((), ())),
                               preferred_element_type=jnp.float32)

    def softplus_parts(z):
        sp = jnp.where(z > SOFTPLUS_LINEAR, z, jnp.log(1.0 + jnp.exp(z)))
        hi = sp.astype(jnp.bfloat16)
        return hi, (sp - hi.astype(jnp.float32)).astype(jnp.bfloat16)

    tri = tri_ref[...]
    starts = [pl.multiple_of(jnp.maximum(qi * n_sub + r - 1, 0) * K_BLOCK, K_BLOCK)
              for r in range(n_sub)]
    zs, belows, accs, totals = [], [], [], []
    for r in range(n_sub):
        z = scores(q_stack[r], k_ref[pl.ds(starts[r], window), :])
        zs.append(jnp.concatenate(
            [z[:, :K_BLOCK], jnp.where(causal, z[:, K_BLOCK:], MASKED_SCORE)], axis=1))
    for r in range(n_sub):
        hi, lo = softplus_parts(zs[r])
        belows.append(jnp.dot(hi, tri, preferred_element_type=jnp.float32)
                      + jnp.dot(lo, tri, preferred_element_type=jnp.float32))
    for r in range(n_sub):
        a = jnp.exp(zs[r] - belows[r]).astype(jnp.bfloat16)
        accs.append(jnp.dot(a, v_ref[pl.ds(starts[r], window), :],
                            preferred_element_type=jnp.float32))
        totals.append(belows[r][:, 0:1])

    def write_out(r, acc_r):
        o_ref[r * K_BLOCK:(r + 1) * K_BLOCK, :] = jnp.where(
            lane < HEAD_DIM, acc_r[:K_BLOCK], acc_r[K_BLOCK:]).astype(o_ref.dtype)

    def walk_on(r):
        diag = qi * n_sub + r

        def cond(carry):
            t, gone, _, _ = carry
            return (t <= diag) & (gone == 0)

        def body(carry):
            t, _, run, acc = carry
            start = pl.multiple_of((diag - t) * K_BLOCK, K_BLOCK)
            z = scores(q_stack[r], k_ref[pl.ds(start, K_BLOCK), :])
            hi, lo = softplus_parts(z)
            sums = jnp.dot(jnp.concatenate([hi, lo], axis=1), scan_ref[...],
                           preferred_element_type=jnp.float32)
            a = jnp.exp(z - (run + sums[:, :K_BLOCK])).astype(jnp.bfloat16)
            acc = acc + jnp.dot(a, v_ref[pl.ds(start, K_BLOCK), :],
                                preferred_element_type=jnp.float32)
            run = run + sums[:, K_BLOCK:]
            return t + 1, (jnp.min(run) > STICK_GONE).astype(jnp.int32), run, acc

        run = jnp.broadcast_to(totals[r], (chain_rows, K_BLOCK))
        _, _, _, acc = lax.while_loop(cond, body, (jnp.int32(2), jnp.int32(0), run, accs[r]))
        write_out(r, acc)

    for r in range(n_sub):
        write_out(r, accs[r])
        pl.when(jnp.min(totals[r]) <= STICK_GONE)(lambda r=r: walk_on(r))


def _scan_matrices():
    j = lax.broadcasted_iota(jnp.int32, (2 * K_BLOCK, 2 * K_BLOCK), 0)
    s = lax.broadcasted_iota(jnp.int32, (2 * K_BLOCK, 2 * K_BLOCK), 1)
    tri = (j >= s).astype(jnp.bfloat16)
    scan = ((s >= K_BLOCK) | (j % K_BLOCK >= s)).astype(jnp.bfloat16)
    return tri, scan


def _attention(q, kv, batch, l_pad):
    q4 = q.reshape(N_HEAD_TILES, batch, l_pad, LANES)
    kv4 = kv.reshape(2 * N_HEAD_TILES, batch, l_pad, LANES)
    tri, scan = _scan_matrices()
    q_spec = pl.BlockSpec((None, None, Q_BLOCK, LANES), lambda b, p, i: (p, b, i, 0))
    out = pl.pallas_call(
        _attention_kernel,
        out_shape=jax.ShapeDtypeStruct(q4.shape, jnp.bfloat16),
        grid=(batch, N_HEAD_TILES, l_pad // Q_BLOCK),
        in_specs=[q_spec,
                  pl.BlockSpec((None, None, l_pad, LANES), lambda b, p, i: (p, b, 0, 0)),
                  pl.BlockSpec((None, None, l_pad, LANES),
                               lambda b, p, i: (N_HEAD_TILES + p, b, 0, 0)),
                  _resident(tri.shape), _resident(scan.shape)],
        out_specs=q_spec,
        compiler_params=_params(("parallel", "parallel", "arbitrary")),
        name="stick_breaking_attention",
    )(q4, kv4, kv4, tri, scan)
    return out.reshape(N_HEAD_TILES, batch * l_pad, LANES)


def kernel(x, meta_tokens, norm_gains, conv_in_proj, conv_w, conv_out_proj, kv_norm,
           w_k, w_v, w_q, w_o, mlp_w1, mlp_w2):
    batch, seq, _ = x.shape
    depth = norm_gains.shape[0]
    n_conv = conv_in_proj.shape[0]
    first = FRONT_PAD
    last = first + N_META + seq
    l_pad = -(-(last + CONV_WIDTH - 1) // Q_BLOCK) * Q_BLOCK
    while (batch * l_pad) % ROW_BLOCK:
        l_pad += Q_BLOCK
    bf16 = jnp.bfloat16

    meta = jnp.broadcast_to(meta_tokens[None].astype(x.dtype), (batch, N_META, D_MODEL))
    h = jnp.concatenate([jnp.zeros((batch, first, D_MODEL), x.dtype), meta, x,
                         jnp.zeros((batch, l_pad - last, D_MODEL), x.dtype)], axis=1)
    h = h.reshape(batch * l_pad, D_MODEL)

    kv = None
    for layer in range(depth):
        g = norm_gains[layer]
        w1, w2 = mlp_w1[layer].astype(bf16), mlp_w2[layer].astype(bf16)
        if layer < n_conv:
            h = _conv_mixer(h, g, conv_in_proj[layer].astype(bf16), conv_w[layer],
                            conv_out_proj[layer].astype(bf16))
            h = _mlp(h, g, w1, w2)
        else:
            if layer == n_conv:
                w_kv = jnp.concatenate([w_k, w_v], axis=1).astype(bf16)
                kv = _norm_proj(h, kv_norm[None, :], w_kv)
            j = layer - n_conv
            w_q_scaled = (w_q[j] * HEAD_DIM ** -0.5).astype(bf16)
            q = _norm_proj(h, g[0:1], w_q_scaled)
            o = _attention(q, kv, batch, l_pad)
            h = _attn_mlp(h, o, g, w_o[j].astype(bf16), w1, w2)
    return h.reshape(batch, l_pad, D_MODEL)[:, first + N_META:last]
```

```python
import jax
import jax.numpy as jnp
from jax import lax
from jax.experimental import pallas as pl
from jax.experimental.pallas import tpu as pltpu

D_MODEL = 1024
N_META = 16
N_HEADS = 16
HEAD_DIM = D_MODEL // N_HEADS
D_FF = 4 * D_MODEL
CONV_WIDTH = 3
RMS_EPS = 1e-6

LANES = 128
SUBLANES = 8
HEADS_PER_TILE = LANES // HEAD_DIM
N_HEAD_TILES = D_MODEL // LANES

ROW_BLOCK = 512
K_BLOCK = 128
Q_BLOCK = 6 * K_BLOCK
FRONT_PAD = K_BLOCK
FF_CHUNK = 1024
VMEM_LIMIT_BYTES = 56 * 1024 * 1024
MASKED_SCORE = -1e30
STICK_GONE = 90.0
SOFTPLUS_LINEAR = 20.0


def _rms_norm(x, g):
    ms = jnp.mean(x * x, axis=-1, keepdims=True)
    return (x * lax.rsqrt(ms + RMS_EPS)) * g


def _resident(shape):
    return pl.BlockSpec(shape, lambda *_: (0,) * len(shape), pipeline_mode=pl.Buffered(1))


def _params(semantics):
    return pltpu.CompilerParams(dimension_semantics=semantics,
                                vmem_limit_bytes=VMEM_LIMIT_BYTES)


def _conv_mixer_kernel(h_ref, g_ref, w_in_ref, cw_ref, w_out_ref, o_ref, u_ref):
    rows = h_ref.shape[0]

    @pl.when(pl.program_id(0) == 0)
    def _():
        u_ref[0:SUBLANES, :] = jnp.zeros((SUBLANES, D_MODEL), jnp.float32)

    h = h_ref[...]
    n = _rms_norm(h, g_ref[0:1, :]).astype(jnp.bfloat16)
    c_gate = jnp.dot(n, w_in_ref[:, D_MODEL:2 * D_MODEL], preferred_element_type=jnp.float32)
    xv = jnp.dot(n, w_in_ref[:, 2 * D_MODEL:], preferred_element_type=jnp.float32)
    u = c_gate * xv
    u_ref[SUBLANES:SUBLANES + rows, :] = u
    conv = (u_ref[SUBLANES - 2:SUBLANES - 2 + rows, :] * cw_ref[0:1, :]
            + u_ref[SUBLANES - 1:SUBLANES - 1 + rows, :] * cw_ref[1:2, :]
            + u * cw_ref[2:3, :])
    u_ref[0:SUBLANES, :] = u_ref[rows:rows + SUBLANES, :]
    b_gate = jnp.dot(n, w_in_ref[:, :D_MODEL], preferred_element_type=jnp.float32)
    mix = jnp.dot((b_gate * conv).astype(jnp.bfloat16), w_out_ref[...],
                  preferred_element_type=jnp.float32)
    o_ref[...] = h + _rms_norm(mix, g_ref[1:2, :])


def _conv_mixer(h, gains, w_in, conv_w, w_out):
    rows = h.shape[0]
    row_spec = pl.BlockSpec((ROW_BLOCK, D_MODEL), lambda i: (i, 0))
    return pl.pallas_call(
        _conv_mixer_kernel,
        out_shape=jax.ShapeDtypeStruct(h.shape, h.dtype),
        grid=(rows // ROW_BLOCK,),
        in_specs=[row_spec, _resident(gains.shape), _resident(w_in.shape),
                  _resident(conv_w.shape), _resident(w_out.shape)],
        out_specs=row_spec,
        scratch_shapes=[pltpu.VMEM((ROW_BLOCK + SUBLANES, D_MODEL), jnp.float32)],
        compiler_params=_params(("arbitrary",)),
        name="conv_mixer",
    )(h, gains, w_in, conv_w, w_out)


def _mlp_residual(h, g_ref, w1_ref, w2_ref):
    n = _rms_norm(h, g_ref[2:3, :]).astype(jnp.bfloat16)
    ff = jnp.zeros(h.shape, jnp.float32)
    for c in range(D_FF // FF_CHUNK):
        cols = slice(c * FF_CHUNK, (c + 1) * FF_CHUNK)
        a = jnp.maximum(jnp.dot(n, w1_ref[:, cols], preferred_element_type=jnp.float32), 0.0)
        ff = ff + jnp.dot((a * a).astype(jnp.bfloat16), w2_ref[cols, :],
                          preferred_element_type=jnp.float32)
    return h + _rms_norm(ff, g_ref[3:4, :])


def _mlp_kernel(h_ref, g_ref, w1_ref, w2_ref, out_ref):
    out_ref[...] = _mlp_residual(h_ref[...], g_ref, w1_ref, w2_ref)


def _mlp(h, gains, w1, w2):
    rows = h.shape[0]
    row_spec = pl.BlockSpec((ROW_BLOCK, D_MODEL), lambda i: (i, 0))
    return pl.pallas_call(
        _mlp_kernel,
        out_shape=jax.ShapeDtypeStruct(h.shape, h.dtype),
        grid=(rows // ROW_BLOCK,),
        in_specs=[row_spec, _resident(gains.shape), _resident(w1.shape), _resident(w2.shape)],
        out_specs=row_spec,
        compiler_params=_params(("parallel",)),
        name="mlp",
    )(h, gains, w1, w2)


def _attn_mlp_kernel(h_ref, o_ref, g_ref, wo_ref, w1_ref, w2_ref, out_ref):
    o = jnp.concatenate([o_ref[c] for c in range(N_HEAD_TILES)], axis=1)
    mix = jnp.dot(o, wo_ref[...], preferred_element_type=jnp.float32)
    h = h_ref[...] + _rms_norm(mix, g_ref[1:2, :])
    out_ref[...] = _mlp_residual(h, g_ref, w1_ref, w2_ref)


def _attn_mlp(h, o, gains, w_o, w1, w2):
    rows = h.shape[0]
    row_spec = pl.BlockSpec((ROW_BLOCK, D_MODEL), lambda i: (i, 0))
    return pl.pallas_call(
        _attn_mlp_kernel,
        out_shape=jax.ShapeDtypeStruct(h.shape, h.dtype),
        grid=(rows // ROW_BLOCK,),
        in_specs=[row_spec, pl.BlockSpec((N_HEAD_TILES, ROW_BLOCK, LANES), lambda i: (0, i, 0)),
                  _resident(gains.shape), _resident(w_o.shape), _resident(w1.shape),
                  _resident(w2.shape)],
        out_specs=row_spec,
        compiler_params=_params(("parallel",)),
        name="attn_mlp",
    )(h, o, gains, w_o, w1, w2)


def _norm_proj_kernel(h_ref, g_ref, w_ref, o_ref):
    n = _rms_norm(h_ref[...], g_ref[...]).astype(jnp.bfloat16)
    y = jnp.dot(n, w_ref[...], preferred_element_type=jnp.float32).astype(o_ref.dtype)
    for c in range(o_ref.shape[0]):
        o_ref[c] = y[:, c * LANES:(c + 1) * LANES]


def _norm_proj(h, gain, w):
    rows = h.shape[0]
    n_tiles = w.shape[1] // LANES
    return pl.pallas_call(
        _norm_proj_kernel,
        out_shape=jax.ShapeDtypeStruct((n_tiles, rows, LANES), jnp.bfloat16),
        grid=(rows // ROW_BLOCK,),
        in_specs=[pl.BlockSpec((ROW_BLOCK, D_MODEL), lambda i: (i, 0)),
                  _resident(gain.shape), _resident(w.shape)],
        out_specs=pl.BlockSpec((n_tiles, ROW_BLOCK, LANES), lambda i: (0, i, 0)),
        compiler_params=_params(("parallel",)),
        name="norm_proj",
    )(h, gain, w)


def _attention_kernel(q_ref, k_ref, v_ref, tri_ref, scan_ref, o_ref):
    qi = pl.program_id(2)
    n_sub = Q_BLOCK // K_BLOCK
    window = 2 * K_BLOCK
    chain_rows = HEADS_PER_TILE * K_BLOCK
    lane = lax.broadcasted_iota(jnp.int32, (K_BLOCK, LANES), 1)
    row = lax.broadcasted_iota(jnp.int32, (chain_rows, K_BLOCK), 0) % K_BLOCK
    col = lax.broadcasted_iota(jnp.int32, (chain_rows, K_BLOCK), 1)
    causal = col < row

    q_stack = []
    for r in range(n_sub):
        q_sub = q_ref[r * K_BLOCK:(r + 1) * K_BLOCK, :].astype(jnp.float32)
        q_stack.append(jnp.concatenate(
            [jnp.where(lane // HEAD_DIM == head, q_sub, 0.0).astype(jnp.bfloat16)
             for head in range(HEADS_PER_TILE)], axis=0))

    def scores(q_rows, k_rows):
        return lax.dot_general(q_rows, k_rows, (((1,), (1,)), ((), ())),
                               preferred_element_type=jnp.float32)

    def softplus_parts(z):
        sp = jnp.where(z > SOFTPLUS_LINEAR, z, jnp.log(1.0 + jnp.exp(z)))
        hi = sp.astype(jnp.bfloat16)
        return hi, (sp - hi.astype(jnp.float32)).astype(jnp.bfloat16)

    def tile_sums(z):
        hi, lo = softplus_parts(z)
        return jnp.dot(jnp.concatenate([hi, lo], axis=1), scan_ref[...],
                       preferred_element_type=jnp.float32)

    def tile_weights(z, run, sums):
        return jnp.exp(z - (run + sums[:, :K_BLOCK])).astype(jnp.bfloat16)

    tri = tri_ref[...]
    starts = [pl.multiple_of(jnp.maximum(qi * n_sub + r - 1, 0) * K_BLOCK, K_BLOCK)
              for r in range(n_sub)]
    starts3 = [pl.multiple_of(jnp.maximum(qi * n_sub + r - 2, 0) * K_BLOCK, K_BLOCK)
               for r in range(n_sub)]
    zs, z3s, belows, sums3, accs, runs = [], [], [], [], [], []
    for r in range(n_sub):
        z = scores(q_stack[r], k_ref[pl.ds(starts[r], window), :])
        zs.append(jnp.concatenate(
            [z[:, :K_BLOCK], jnp.where(causal, z[:, K_BLOCK:], MASKED_SCORE)], axis=1))
    for r in range(n_sub):
        z3s.append(scores(q_stack[r], k_ref[pl.ds(starts3[r], K_BLOCK), :]))
    for r in range(n_sub):
        hi, lo = softplus_parts(zs[r])
        belows.append(jnp.dot(hi, tri, preferred_element_type=jnp.float32)
                      + jnp.dot(lo, tri, preferred_element_type=jnp.float32))
    for r in range(n_sub):
        sums3.append(tile_sums(z3s[r]))
    for r in range(n_sub):
        a = jnp.exp(zs[r] - belows[r]).astype(jnp.bfloat16)
        accs.append(jnp.dot(a, v_ref[pl.ds(starts[r], window), :],
                            preferred_element_type=jnp.float32))
    for r in range(n_sub):
        total = belows[r][:, 0:1]
        a = tile_weights(z3s[r], total, sums3[r])
        accs[r] = accs[r] + jnp.dot(a, v_ref[pl.ds(starts3[r], K_BLOCK), :],
                                    preferred_element_type=jnp.float32)
        runs.append(total + sums3[r][:, K_BLOCK:])

    def write_out(r, acc_r):
        o_ref[r * K_BLOCK:(r + 1) * K_BLOCK, :] = jnp.where(
            lane < HEAD_DIM, acc_r[:K_BLOCK], acc_r[K_BLOCK:]).astype(o_ref.dtype)

    def walk_on(r):
        diag = qi * n_sub + r

        def cond(carry):
            t, gone, _, _ = carry
            return (t <= diag) & (gone == 0)

        def body(carry):
            t, _, run, acc = carry
            start = pl.multiple_of((diag - t) * K_BLOCK, K_BLOCK)
            z = scores(q_stack[r], k_ref[pl.ds(start, K_BLOCK), :])
            sums = tile_sums(z)
            acc = acc + jnp.dot(tile_weights(z, run, sums), v_ref[pl.ds(start, K_BLOCK), :],
                                preferred_element_type=jnp.float32)
            run = run + sums[:, K_BLOCK:]
            return t + 1, (jnp.min(run) > STICK_GONE).astype(jnp.int32), run, acc

        _, _, _, acc = lax.while_loop(cond, body, (jnp.int32(3), jnp.int32(0), runs[r], accs[r]))
        write_out(r, acc)

    for r in range(n_sub):
        write_out(r, accs[r])
        pl.when(jnp.min(runs[r]) <= STICK_GONE)(lambda r=r: walk_on(r))


def _scan_matrices():
    j = lax.broadcasted_iota(jnp.int32, (2 * K_BLOCK, 2 * K_BLOCK), 0)
    s = lax.broadcasted_iota(jnp.int32, (2 * K_BLOCK, 2 * K_BLOCK), 1)
    tri = (j >= s).astype(jnp.bfloat16)
    scan = ((s >= K_BLOCK) | (j % K_BLOCK >= s)).astype(jnp.bfloat16)
    return tri, scan


def _attention(q, kv, batch, l_pad):
    q4 = q.reshape(N_HEAD_TILES, batch, l_pad, LANES)
    kv4 = kv.reshape(2 * N_HEAD_TILES, batch, l_pad, LANES)
    tri, scan = _scan_matrices()
    q_spec = pl.BlockSpec((None, None, Q_BLOCK, LANES), lambda b, p, i: (p, b, i, 0))
    out = pl.pallas_call(
        _attention_kernel,
        out_shape=jax.ShapeDtypeStruct(q4.shape, jnp.bfloat16),
        grid=(batch, N_HEAD_TILES, l_pad // Q_BLOCK),
        in_specs=[q_spec,
                  pl.BlockSpec((None, None, l_pad, LANES), lambda b, p, i: (p, b, 0, 0)),
                  pl.BlockSpec((None, None, l_pad, LANES),
                               lambda b, p, i: (N_HEAD_TILES + p, b, 0, 0)),
                  _resident(tri.shape), _resident(scan.shape)],
        out_specs=q_spec,
        compiler_params=_params(("parallel", "parallel", "arbitrary")),
        name="stick_breaking_attention",
    )(q4, kv4, kv4, tri, scan)
    return out.reshape(N_HEAD_TILES, batch * l_pad, LANES)


def kernel(x, meta_tokens, norm_gains, conv_in_proj, conv_w, conv_out_proj, kv_norm,
           w_k, w_v, w_q, w_o, mlp_w1, mlp_w2):
    batch, seq, _ = x.shape
    depth = norm_gains.shape[0]
    n_conv = conv_in_proj.shape[0]
    first = FRONT_PAD
    last = first + N_META + seq
    l_pad = -(-(last + CONV_WIDTH - 1) // Q_BLOCK) * Q_BLOCK
    while (batch * l_pad) % ROW_BLOCK:
        l_pad += Q_BLOCK
    bf16 = jnp.bfloat16

    meta = jnp.broadcast_to(meta_tokens[None].astype(x.dtype), (batch, N_META, D_MODEL))
    h = jnp.concatenate([jnp.zeros((batch, first, D_MODEL), x.dtype), meta, x,
                         jnp.zeros((batch, l_pad - last, D_MODEL), x.dtype)], axis=1)
    h = h.reshape(batch * l_pad, D_MODEL)

    kv = None
    for layer in range(depth):
        g = norm_gains[layer]
        w1, w2 = mlp_w1[layer].astype(bf16), mlp_w2[layer].astype(bf16)
        if layer < n_conv:
            h = _conv_mixer(h, g, conv_in_proj[layer].astype(bf16), conv_w[layer],
                            conv_out_proj[layer].astype(bf16))
            h = _mlp(h, g, w1, w2)
        else:
            if layer == n_conv:
                w_kv = jnp.concatenate([w_k, w_v], axis=1).astype(bf16)
                kv = _norm_proj(h, kv_norm[None, :], w_kv)
            j = layer - n_conv
            w_q_scaled = (w_q[j] * HEAD_DIM ** -0.5).astype(bf16)
            q = _norm_proj(h, g[0:1], w_q_scaled)
            o = _attention(q, kv, batch, l_pad)
            h = _attn_mlp(h, o, g, w_o[j].astype(bf16), w1, w2)
    return h.reshape(batch, l_pad, D_MODEL)[:, first + N_META:last]
```

```python
import jax
import jax.numpy as jnp
from jax import lax
from jax.experimental import pallas as pl
from jax.experimental.pallas import tpu as pltpu

D_MODEL = 1024
N_META = 16
N_HEADS = 16
HEAD_DIM = D_MODEL // N_HEADS
D_FF = 4 * D_MODEL
CONV_WIDTH = 3
RMS_EPS = 1e-6

LANES = 128
SUBLANES = 8
HEADS_PER_TILE = LANES // HEAD_DIM
N_HEAD_TILES = D_MODEL // LANES

ROW_BLOCK = 512
K_BLOCK = 128
Q_BLOCK = 6 * K_BLOCK
FRONT_PAD = K_BLOCK
FF_CHUNK = 1024
VMEM_LIMIT_BYTES = 56 * 1024 * 1024
MASKED_SCORE = -1e30
STICK_GONE = 90.0
SOFTPLUS_LINEAR = 20.0


def _rms_norm(x, g):
    ms = jnp.mean(x * x, axis=-1, keepdims=True)
    return (x * lax.rsqrt(ms + RMS_EPS)) * g


def _resident(shape):
    return pl.BlockSpec(shape, lambda *_: (0,) * len(shape), pipeline_mode=pl.Buffered(1))


def _params(semantics):
    return pltpu.CompilerParams(dimension_semantics=semantics,
                                vmem_limit_bytes=VMEM_LIMIT_BYTES)


def _conv_mixer_kernel(h_ref, g_ref, w_in_ref, cw_ref, w_out_ref, o_ref, u_ref):
    rows = h_ref.shape[0]

    @pl.when(pl.program_id(0) == 0)
    def _():
        u_ref[0:SUBLANES, :] = jnp.zeros((SUBLANES, D_MODEL), jnp.float32)

    h = h_ref[...]
    n = _rms_norm(h, g_ref[0:1, :]).astype(jnp.bfloat16)
    c_gate = jnp.dot(n, w_in_ref[:, D_MODEL:2 * D_MODEL], preferred_element_type=jnp.float32)
    xv = jnp.dot(n, w_in_ref[:, 2 * D_MODEL:], preferred_element_type=jnp.float32)
    u = c_gate * xv
    u_ref[SUBLANES:SUBLANES + rows, :] = u
    conv = (u_ref[SUBLANES - 2:SUBLANES - 2 + rows, :] * cw_ref[0:1, :]
            + u_ref[SUBLANES - 1:SUBLANES - 1 + rows, :] * cw_ref[1:2, :]
            + u * cw_ref[2:3, :])
    u_ref[0:SUBLANES, :] = u_ref[rows:rows + SUBLANES, :]
    b_gate = jnp.dot(n, w_in_ref[:, :D_MODEL], preferred_element_type=jnp.float32)
    mix = jnp.dot((b_gate * conv).astype(jnp.bfloat16), w_out_ref[...],
                  preferred_element_type=jnp.float32)
    o_ref[...] = h + _rms_norm(mix, g_ref[1:2, :])


def _conv_mixer(h, gains, w_in, conv_w, w_out):
    rows = h.shape[0]
    row_spec = pl.BlockSpec((ROW_BLOCK, D_MODEL), lambda i: (i, 0))
    return pl.pallas_call(
        _conv_mixer_kernel,
        out_shape=jax.ShapeDtypeStruct(h.shape, h.dtype),
        grid=(rows // ROW_BLOCK,),
        in_specs=[row_spec, _resident(gains.shape), _resident(w_in.shape),
                  _resident(conv_w.shape), _resident(w_out.shape)],
        out_specs=row_spec,
        scratch_shapes=[pltpu.VMEM((ROW_BLOCK + SUBLANES, D_MODEL), jnp.float32)],
        compiler_params=_params(("arbitrary",)),
        name="conv_mixer",
    )(h, gains, w_in, conv_w, w_out)


def _mlp_residual(h, g_ref, w1_ref, w2_ref):
    n = _rms_norm(h, g_ref[2:3, :]).astype(jnp.bfloat16)
    ff = jnp.zeros(h.shape, jnp.float32)
    for c in range(D_FF // FF_CHUNK):
        cols = slice(c * FF_CHUNK, (c + 1) * FF_CHUNK)
        a = jnp.maximum(jnp.dot(n, w1_ref[:, cols], preferred_element_type=jnp.float32), 0.0)
        ff = ff + jnp.dot((a * a).astype(jnp.bfloat16), w2_ref[cols, :],
                          preferred_element_type=jnp.float32)
    return h + _rms_norm(ff, g_ref[3:4, :])


def _mlp_kernel(h_ref, g_ref, w1_ref, w2_ref, out_ref):
    out_ref[...] = _mlp_residual(h_ref[...], g_ref, w1_ref, w2_ref)


def _mlp(h, gains, w1, w2):
    rows = h.shape[0]
    row_spec = pl.BlockSpec((ROW_BLOCK, D_MODEL), lambda i: (i, 0))
    return pl.pallas_call(
        _mlp_kernel,
        out_shape=jax.ShapeDtypeStruct(h.shape, h.dtype),
        grid=(rows // ROW_BLOCK,),
        in_specs=[row_spec, _resident(gains.shape), _resident(w1.shape), _resident(w2.shape)],
        out_specs=row_spec,
        compiler_params=_params(("parallel",)),
        name="mlp",
    )(h, gains, w1, w2)


def _attn_mlp_kernel(h_ref, o_ref, g_ref, wo_ref, w1_ref, w2_ref, out_ref):
    o = jnp.concatenate([o_ref[c] for c in range(N_HEAD_TILES)], axis=1)
    mix = jnp.dot(o, wo_ref[...], preferred_element_type=jnp.float32)
    h = h_ref[...] + _rms_norm(mix, g_ref[1:2, :])
    out_ref[...] = _mlp_residual(h, g_ref, w1_ref, w2_ref)


def _attn_mlp(h, o, gains, w_o, w1, w2):
    rows = h.shape[0]
    row_spec = pl.BlockSpec((ROW_BLOCK, D_MODEL), lambda i: (i, 0))
    return pl.pallas_call(
        _attn_mlp_kernel,
        out_shape=jax.ShapeDtypeStruct(h.shape, h.dtype),
        grid=(rows // ROW_BLOCK,),
        in_specs=[row_spec, pl.BlockSpec((N_HEAD_TILES, ROW_BLOCK, LANES), lambda i: (0, i, 0)),
                  _resident(gains.shape), _resident(w_o.shape), _resident(w1.shape),
                  _resident(w2.shape)],
        out_specs=row_spec,
        compiler_params=_params(("parallel",)),
        name="attn_mlp",
    )(h, o, gains, w_o, w1, w2)


def _attn_mlp_real_rows(h, o, gains, w_o, w1, w2, batch, l_pad, first_real, seq):
    blocks = seq // ROW_BLOCK
    bf16_rows = 2 * SUBLANES
    assert l_pad % bf16_rows == 0 and first_real % bf16_rows == 0

    def row_start(b, i):
        return pl.multiple_of(b * l_pad + first_real + i * ROW_BLOCK, bf16_rows)

    return pl.pallas_call(
        _attn_mlp_kernel,
        out_shape=jax.ShapeDtypeStruct((batch * seq, D_MODEL), h.dtype),
        grid=(batch, blocks),
        in_specs=[pl.BlockSpec((pl.Element(ROW_BLOCK), pl.Element(D_MODEL)),
                               lambda b, i: (row_start(b, i), 0)),
                  pl.BlockSpec((pl.Element(N_HEAD_TILES), pl.Element(ROW_BLOCK),
                                pl.Element(LANES)),
                               lambda b, i: (0, row_start(b, i), 0)),
                  _resident(gains.shape), _resident(w_o.shape), _resident(w1.shape),
                  _resident(w2.shape)],
        out_specs=pl.BlockSpec((ROW_BLOCK, D_MODEL), lambda b, i: (b * blocks + i, 0)),
        compiler_params=_params(("parallel", "parallel")),
        name="attn_mlp_out",
    )(h, o, gains, w_o, w1, w2)


def _norm_proj_kernel(h_ref, g_ref, w_ref, o_ref):
    n = _rms_norm(h_ref[...], g_ref[...]).astype(jnp.bfloat16)
    y = jnp.dot(n, w_ref[...], preferred_element_type=jnp.float32).astype(o_ref.dtype)
    for c in range(o_ref.shape[0]):
        o_ref[c] = y[:, c * LANES:(c + 1) * LANES]


def _norm_proj(h, gain, w):
    rows = h.shape[0]
    n_tiles = w.shape[1] // LANES
    return pl.pallas_call(
        _norm_proj_kernel,
        out_shape=jax.ShapeDtypeStruct((n_tiles, rows, LANES), jnp.bfloat16),
        grid=(rows // ROW_BLOCK,),
        in_specs=[pl.BlockSpec((ROW_BLOCK, D_MODEL), lambda i: (i, 0)),
                  _resident(gain.shape), _resident(w.shape)],
        out_specs=pl.BlockSpec((n_tiles, ROW_BLOCK, LANES), lambda i: (0, i, 0)),
        compiler_params=_params(("parallel",)),
        name="norm_proj",
    )(h, gain, w)


def _attention_kernel(q_ref, k_ref, v_ref, tri_ref, scan_ref, o_ref):
    qi = pl.program_id(2)
    n_sub = Q_BLOCK // K_BLOCK
    window = 2 * K_BLOCK
    chain_rows = HEADS_PER_TILE * K_BLOCK
    lane = lax.broadcasted_iota(jnp.int32, (K_BLOCK, LANES), 1)
    row = lax.broadcasted_iota(jnp.int32, (chain_rows, K_BLOCK), 0) % K_BLOCK
    col = lax.broadcasted_iota(jnp.int32, (chain_rows, K_BLOCK), 1)
    causal = col < row

    q_stack = []
    for r in range(n_sub):
        q_sub = q_ref[r * K_BLOCK:(r + 1) * K_BLOCK, :].astype(jnp.float32)
        q_stack.append(jnp.concatenate(
            [jnp.where(lane // HEAD_DIM == head, q_sub, 0.0).astype(jnp.bfloat16)
             for head in range(HEADS_PER_TILE)], axis=0))

    def scores(q_rows, k_rows):
        return lax.dot_general(q_rows, k_rows, (((1,), (1,)), ((), ())),
                               preferred_element_type=jnp.float32)

    def softplus_parts(z):
        sp = jnp.where(z > SOFTPLUS_LINEAR, z, jnp.log(1.0 + jnp.exp(z)))
        hi = sp.astype(jnp.bfloat16)
        return hi, (sp - hi.astype(jnp.float32)).astype(jnp.bfloat16)

    def tile_sums(z):
        hi, lo = softplus_parts(z)
        return jnp.dot(jnp.concatenate([hi, lo], axis=1), scan_ref[...],
                       preferred_element_type=jnp.float32)

    def tile_weights(z, run, sums):
        return jnp.exp(z - (run + sums[:, :K_BLOCK])).astype(jnp.bfloat16)

    tri = tri_ref[...]
    starts = [pl.multiple_of(jnp.maximum(qi * n_sub + r - 1, 0) * K_BLOCK, K_BLOCK)
              for r in range(n_sub)]
    starts3 = [pl.multiple_of(jnp.maximum(qi * n_sub + r - 2, 0) * K_BLOCK, K_BLOCK)
               for r in range(n_sub)]
    zs, z3s, belows, sums3, accs, runs = [], [], [], [], [], []
    for r in range(n_sub):
        z = scores(q_stack[r], k_ref[pl.ds(starts[r], window), :])
        zs.append(jnp.concatenate(
            [z[:, :K_BLOCK], jnp.where(causal, z[:, K_BLOCK:], MASKED_SCORE)], axis=1))
    for r in range(n_sub):
        z3s.append(scores(q_stack[r], k_ref[pl.ds(starts3[r], K_BLOCK), :]))
    for r in range(n_sub):
        hi, lo = softplus_parts(zs[r])
        belows.append(jnp.dot(hi, tri, preferred_element_type=jnp.float32)
                      + jnp.dot(lo, tri, preferred_element_type=jnp.float32))
    for r in range(n_sub):
        sums3.append(tile_sums(z3s[r]))
    for r in range(n_sub):
        a = jnp.exp(zs[r] - belows[r]).astype(jnp.bfloat16)
        accs.append(jnp.dot(a, v_ref[pl.ds(starts[r], window), :],
                            preferred_element_type=jnp.float32))
    for r in range(n_sub):
        total = belows[r][:, 0:1]
        a = tile_weights(z3s[r], total, sums3[r])
        accs[r] = accs[r] + jnp.dot(a, v_ref[pl.ds(starts3[r], K_BLOCK), :],
                                    preferred_element_type=jnp.float32)
        runs.append(total + sums3[r][:, K_BLOCK:])

    def write_out(r, acc_r):
        o_ref[r * K_BLOCK:(r + 1) * K_BLOCK, :] = jnp.where(
            lane < HEAD_DIM, acc_r[:K_BLOCK], acc_r[K_BLOCK:]).astype(o_ref.dtype)

    def walk_on(r):
        diag = qi * n_sub + r

        def cond(carry):
            t, gone, _, _ = carry
            return (t <= diag) & (gone == 0)

        def body(carry):
            t, _, run, acc = carry
            start = pl.multiple_of((diag - t) * K_BLOCK, K_BLOCK)
            z = scores(q_stack[r], k_ref[pl.ds(start, K_BLOCK), :])
            sums = tile_sums(z)
            acc = acc + jnp.dot(tile_weights(z, run, sums), v_ref[pl.ds(start, K_BLOCK), :],
                                preferred_element_type=jnp.float32)
            run = run + sums[:, K_BLOCK:]
            return t + 1, (jnp.min(run) > STICK_GONE).astype(jnp.int32), run, acc

        _, _, _, acc = lax.while_loop(cond, body, (jnp.int32(3), jnp.int32(0), runs[r], accs[r]))
        write_out(r, acc)

    for r in range(n_sub):
        write_out(r, accs[r])
    least = runs[0]
    for r in range(1, n_sub):
        least = jnp.minimum(least, runs[r])

    @pl.when(jnp.min(least) <= STICK_GONE)
    def _():
        for r in range(n_sub):
            pl.when(jnp.min(runs[r]) <= STICK_GONE)(lambda r=r: walk_on(r))


def _scan_matrices():
    j = lax.broadcasted_iota(jnp.int32, (2 * K_BLOCK, 2 * K_BLOCK), 0)
    s = lax.broadcasted_iota(jnp.int32, (2 * K_BLOCK, 2 * K_BLOCK), 1)
    tri = (j >= s).astype(jnp.bfloat16)
    scan = ((s >= K_BLOCK) | (j % K_BLOCK >= s)).astype(jnp.bfloat16)
    return tri, scan


def _attention(q, kv, batch, l_pad):
    q4 = q.reshape(N_HEAD_TILES, batch, l_pad, LANES)
    kv4 = kv.reshape(2 * N_HEAD_TILES, batch, l_pad, LANES)
    tri, scan = _scan_matrices()
    q_spec = pl.BlockSpec((None, None, Q_BLOCK, LANES), lambda b, p, i: (p, b, i, 0))
    out = pl.pallas_call(
        _attention_kernel,
        out_shape=jax.ShapeDtypeStruct(q4.shape, jnp.bfloat16),
        grid=(batch, N_HEAD_TILES, l_pad // Q_BLOCK),
        in_specs=[q_spec,
                  pl.BlockSpec((None, None, l_pad, LANES), lambda b, p, i: (p, b, 0, 0)),
                  pl.BlockSpec((None, None, l_pad, LANES),
                               lambda b, p, i: (N_HEAD_TILES + p, b, 0, 0)),
                  _resident(tri.shape), _resident(scan.shape)],
        out_specs=q_spec,
        compiler_params=_params(("parallel", "parallel", "arbitrary")),
        name="stick_breaking_attention",
    )(q4, kv4, kv4, tri, scan)
    return out.reshape(N_HEAD_TILES, batch * l_pad, LANES)


def kernel(x, meta_tokens, norm_gains, conv_in_proj, conv_w, conv_out_proj, kv_norm,
           w_k, w_v, w_q, w_o, mlp_w1, mlp_w2):
    batch, seq, _ = x.shape
    depth = norm_gains.shape[0]
    n_conv = conv_in_proj.shape[0]
    first = FRONT_PAD
    last = first + N_META + seq
    l_pad = -(-(last + CONV_WIDTH - 1) // Q_BLOCK) * Q_BLOCK
    while (batch * l_pad) % ROW_BLOCK:
        l_pad += Q_BLOCK
    bf16 = jnp.bfloat16

    meta = jnp.broadcast_to(meta_tokens[None].astype(x.dtype), (batch, N_META, D_MODEL))
    h = jnp.concatenate([jnp.zeros((batch, first, D_MODEL), x.dtype), meta, x,
                         jnp.zeros((batch, l_pad - last, D_MODEL), x.dtype)], axis=1)
    h = h.reshape(batch * l_pad, D_MODEL)

    kv = None
    for layer in range(depth):
        g = norm_gains[layer]
        w1, w2 = mlp_w1[layer].astype(bf16), mlp_w2[layer].astype(bf16)
        if layer < n_conv:
            h = _conv_mixer(h, g, conv_in_proj[layer].astype(bf16), conv_w[layer],
                            conv_out_proj[layer].astype(bf16))
            h = _mlp(h, g, w1, w2)
        else:
            if layer == n_conv:
                w_kv = jnp.concatenate([w_k, w_v], axis=1).astype(bf16)
                kv = _norm_proj(h, kv_norm[None, :], w_kv)
            j = layer - n_conv
            w_q_scaled = (w_q[j] * HEAD_DIM ** -0.5).astype(bf16)
            q = _norm_proj(h, g[0:1], w_q_scaled)
            o = _attention(q, kv, batch, l_pad)
            if layer == depth - 1 and seq % ROW_BLOCK == 0:
                out = _attn_mlp_real_rows(h, o, g, w_o[j].astype(bf16), w1, w2,
                                          batch, l_pad, first + N_META, seq)
                return out.reshape(batch, seq, D_MODEL)
            h = _attn_mlp(h, o, g, w_o[j].astype(bf16), w1, w2)
    return h.reshape(batch, l_pad, D_MODEL)[:, first + N_META:last]
```

```python
import jax
import jax.numpy as jnp
from jax import lax
from jax.experimental import pallas as pl
from jax.experimental.pallas import tpu as pltpu

D_MODEL = 1024
N_META = 16
N_HEADS = 16
HEAD_DIM = D_MODEL // N_HEADS
D_FF = 4 * D_MODEL
CONV_WIDTH = 3
RMS_EPS = 1e-6

LANES = 128
SUBLANES = 8
HEADS_PER_TILE = LANES // HEAD_DIM
N_HEAD_TILES = D_MODEL // LANES

ROW_BLOCK = 512
K_BLOCK = 128
Q_BLOCK = 6 * K_BLOCK
FRONT_PAD = K_BLOCK
FF_CHUNK = 1024
VMEM_LIMIT_BYTES = 56 * 1024 * 1024
MASKED_SCORE = -1e30
STICK_GONE = 90.0
SOFTPLUS_LINEAR = 20.0


def _rms_norm(x, g):
    ms = jnp.mean(x * x, axis=-1, keepdims=True)
    return (x * lax.rsqrt(ms + RMS_EPS)) * g


def _resident(shape):
    return pl.BlockSpec(shape, lambda *_: (0,) * len(shape), pipeline_mode=pl.Buffered(1))


def _params(semantics):
    return pltpu.CompilerParams(dimension_semantics=semantics,
                                vmem_limit_bytes=VMEM_LIMIT_BYTES)


def _conv_mixer_kernel(h_ref, g_ref, w_in_ref, cw_ref, w_out_ref, w1_ref, w2_ref, o_ref, u_ref):
    rows = h_ref.shape[0]

    @pl.when(pl.program_id(0) == 0)
    def _():
        u_ref[0:SUBLANES, :] = jnp.zeros((SUBLANES, D_MODEL), jnp.float32)

    h = h_ref[...]
    n = _rms_norm(h, g_ref[0:1, :]).astype(jnp.bfloat16)
    c_gate = jnp.dot(n, w_in_ref[:, D_MODEL:2 * D_MODEL], preferred_element_type=jnp.float32)
    xv = jnp.dot(n, w_in_ref[:, 2 * D_MODEL:], preferred_element_type=jnp.float32)
    u = c_gate * xv
    u_ref[SUBLANES:SUBLANES + rows, :] = u
    conv = (u_ref[SUBLANES - 2:SUBLANES - 2 + rows, :] * cw_ref[0:1, :]
            + u_ref[SUBLANES - 1:SUBLANES - 1 + rows, :] * cw_ref[1:2, :]
            + u * cw_ref[2:3, :])
    u_ref[0:SUBLANES, :] = u_ref[rows:rows + SUBLANES, :]
    b_gate = jnp.dot(n, w_in_ref[:, :D_MODEL], preferred_element_type=jnp.float32)
    mix = jnp.dot((b_gate * conv).astype(jnp.bfloat16), w_out_ref[...],
                  preferred_element_type=jnp.float32)
    o_ref[...] = _mlp_residual(h + _rms_norm(mix, g_ref[1:2, :]), g_ref, w1_ref, w2_ref)


def _conv_mixer(h, gains, w_in, conv_w, w_out, w1, w2):
    rows = h.shape[0]
    row_spec = pl.BlockSpec((ROW_BLOCK, D_MODEL), lambda i: (i, 0))
    return pl.pallas_call(
        _conv_mixer_kernel,
        out_shape=jax.ShapeDtypeStruct(h.shape, h.dtype),
        grid=(rows // ROW_BLOCK,),
        in_specs=[row_spec, _resident(gains.shape), _resident(w_in.shape),
                  _resident(conv_w.shape), _resident(w_out.shape), _resident(w1.shape),
                  _resident(w2.shape)],
        out_specs=row_spec,
        scratch_shapes=[pltpu.VMEM((ROW_BLOCK + SUBLANES, D_MODEL), jnp.float32)],
        compiler_params=_params(("arbitrary",)),
        name="conv_mixer_mlp",
    )(h, gains, w_in, conv_w, w_out, w1, w2)


def _mlp_residual(h, g_ref, w1_ref, w2_ref):
    n = _rms_norm(h, g_ref[2:3, :]).astype(jnp.bfloat16)
    ff = jnp.zeros(h.shape, jnp.float32)
    for c in range(D_FF // FF_CHUNK):
        cols = slice(c * FF_CHUNK, (c + 1) * FF_CHUNK)
        a = jnp.maximum(jnp.dot(n, w1_ref[:, cols], preferred_element_type=jnp.float32), 0.0)
        ff = ff + jnp.dot((a * a).astype(jnp.bfloat16), w2_ref[cols, :],
                          preferred_element_type=jnp.float32)
    return h + _rms_norm(ff, g_ref[3:4, :])


def _attn_mlp_kernel(h_ref, o_ref, g_ref, wo_ref, w1_ref, w2_ref, out_ref):
    o = jnp.concatenate([o_ref[c] for c in range(N_HEAD_TILES)], axis=1)
    mix = jnp.dot(o, wo_ref[...], preferred_element_type=jnp.float32)
    h = h_ref[...] + _rms_norm(mix, g_ref[1:2, :])
    out_ref[...] = _mlp_residual(h, g_ref, w1_ref, w2_ref)


def _attn_mlp(h, o, gains, w_o, w1, w2):
    rows = h.shape[0]
    row_spec = pl.BlockSpec((ROW_BLOCK, D_MODEL), lambda i: (i, 0))
    return pl.pallas_call(
        _attn_mlp_kernel,
        out_shape=jax.ShapeDtypeStruct(h.shape, h.dtype),
        grid=(rows // ROW_BLOCK,),
        in_specs=[row_spec, pl.BlockSpec((N_HEAD_TILES, ROW_BLOCK, LANES), lambda i: (0, i, 0)),
                  _resident(gains.shape), _resident(w_o.shape), _resident(w1.shape),
                  _resident(w2.shape)],
        out_specs=row_spec,
        compiler_params=_params(("parallel",)),
        name="attn_mlp",
    )(h, o, gains, w_o, w1, w2)


def _attn_mlp_real_rows(h, o, gains, w_o, w1, w2, batch, l_pad, first_real, seq):
    blocks = seq // ROW_BLOCK
    bf16_rows = 2 * SUBLANES
    assert l_pad % bf16_rows == 0 and first_real % bf16_rows == 0

    def row_start(b, i):
        return pl.multiple_of(b * l_pad + first_real + i * ROW_BLOCK, bf16_rows)

    return pl.pallas_call(
        _attn_mlp_kernel,
        out_shape=jax.ShapeDtypeStruct((batch * seq, D_MODEL), h.dtype),
        grid=(batch, blocks),
        in_specs=[pl.BlockSpec((pl.Element(ROW_BLOCK), pl.Element(D_MODEL)),
                               lambda b, i: (row_start(b, i), 0)),
                  pl.BlockSpec((pl.Element(N_HEAD_TILES), pl.Element(ROW_BLOCK),
                                pl.Element(LANES)),
                               lambda b, i: (0, row_start(b, i), 0)),
                  _resident(gains.shape), _resident(w_o.shape), _resident(w1.shape),
                  _resident(w2.shape)],
        out_specs=pl.BlockSpec((ROW_BLOCK, D_MODEL), lambda b, i: (b * blocks + i, 0)),
        compiler_params=_params(("parallel", "parallel")),
        name="attn_mlp_out",
    )(h, o, gains, w_o, w1, w2)


def _norm_proj_kernel(h_ref, g_ref, w_ref, o_ref):
    n = _rms_norm(h_ref[...], g_ref[...]).astype(jnp.bfloat16)
    y = jnp.dot(n, w_ref[...], preferred_element_type=jnp.float32).astype(o_ref.dtype)
    for c in range(o_ref.shape[0]):
        o_ref[c] = y[:, c * LANES:(c + 1) * LANES]


def _norm_proj(h, gain, w):
    rows = h.shape[0]
    n_tiles = w.shape[1] // LANES
    return pl.pallas_call(
        _norm_proj_kernel,
        out_shape=jax.ShapeDtypeStruct((n_tiles, rows, LANES), jnp.bfloat16),
        grid=(rows // ROW_BLOCK,),
        in_specs=[pl.BlockSpec((ROW_BLOCK, D_MODEL), lambda i: (i, 0)),
                  _resident(gain.shape), _resident(w.shape)],
        out_specs=pl.BlockSpec((n_tiles, ROW_BLOCK, LANES), lambda i: (0, i, 0)),
        compiler_params=_params(("parallel",)),
        name="norm_proj",
    )(h, gain, w)


def _attention_kernel(q_ref, k_ref, v_ref, tri_ref, scan_ref, o_ref):
    qi = pl.program_id(2)
    n_sub = Q_BLOCK // K_BLOCK
    window = 2 * K_BLOCK
    chain_rows = HEADS_PER_TILE * K_BLOCK
    lane = lax.broadcasted_iota(jnp.int32, (K_BLOCK, LANES), 1)
    row = lax.broadcasted_iota(jnp.int32, (chain_rows, K_BLOCK), 0) % K_BLOCK
    col = lax.broadcasted_iota(jnp.int32, (chain_rows, K_BLOCK), 1)
    causal = col < row

    q_stack = []
    for r in range(n_sub):
        q_sub = q_ref[r * K_BLOCK:(r + 1) * K_BLOCK, :].astype(jnp.float32)
        q_stack.append(jnp.concatenate(
            [jnp.where(lane // HEAD_DIM == head, q_sub, 0.0).astype(jnp.bfloat16)
             for head in range(HEADS_PER_TILE)], axis=0))

    def scores(q_rows, k_rows):
        return lax.dot_general(q_rows, k_rows, (((1,), (1,)), ((), ())),
                               preferred_element_type=jnp.float32)

    def softplus_parts(z):
        sp = jnp.where(z > SOFTPLUS_LINEAR, z, jnp.log(1.0 + jnp.exp(z)))
        hi = sp.astype(jnp.bfloat16)
        return hi, (sp - hi.astype(jnp.float32)).astype(jnp.bfloat16)

    def tile_sums(z):
        hi, lo = softplus_parts(z)
        return jnp.dot(jnp.concatenate([hi, lo], axis=1), scan_ref[...],
                       preferred_element_type=jnp.float32)

    def tile_weights(z, run, sums):
        return jnp.exp(z - (run + sums[:, :K_BLOCK])).astype(jnp.bfloat16)

    tri = tri_ref[...]
    starts = [pl.multiple_of(jnp.maximum(qi * n_sub + r - 1, 0) * K_BLOCK, K_BLOCK)
              for r in range(n_sub)]
    starts3 = [pl.multiple_of(jnp.maximum(qi * n_sub + r - 2, 0) * K_BLOCK, K_BLOCK)
               for r in range(n_sub)]
    zs, z3s, belows, sums3, accs, runs = [], [], [], [], [], []
    for r in range(n_sub):
        z = scores(q_stack[r], k_ref[pl.ds(starts[r], window), :])
        zs.append(jnp.concatenate(
            [z[:, :K_BLOCK], jnp.where(causal, z[:, K_BLOCK:], MASKED_SCORE)], axis=1))
    for r in range(n_sub):
        z3s.append(scores(q_stack[r], k_ref[pl.ds(starts3[r], K_BLOCK), :]))
    for r in range(n_sub):
        hi, lo = softplus_parts(zs[r])
        belows.append(jnp.dot(hi, tri, preferred_element_type=jnp.float32)
                      + jnp.dot(lo, tri, preferred_element_type=jnp.float32))
    for r in range(n_sub):
        sums3.append(tile_sums(z3s[r]))
    for r in range(n_sub):
        a = jnp.exp(zs[r] - belows[r]).astype(jnp.bfloat16)
        accs.append(jnp.dot(a, v_ref[pl.ds(starts[r], window), :],
                            preferred_element_type=jnp.float32))
    for r in range(n_sub):
        total = belows[r][:, 0:1]
        a = tile_weights(z3s[r], total, sums3[r])
        accs[r] = accs[r] + jnp.dot(a, v_ref[pl.ds(starts3[r], K_BLOCK), :],
                                    preferred_element_type=jnp.float32)
        runs.append(total + sums3[r][:, K_BLOCK:])

    def write_out(r, acc_r):
        o_ref[r * K_BLOCK:(r + 1) * K_BLOCK, :] = jnp.where(
            lane < HEAD_DIM, acc_r[:K_BLOCK], acc_r[K_BLOCK:]).astype(o_ref.dtype)

    def walk_on(r):
        diag = qi * n_sub + r

        def cond(carry):
            t, gone, _, _ = carry
            return (t <= diag) & (gone == 0)

        def body(carry):
            t, _, run, acc = carry
            start = pl.multiple_of((diag - t) * K_BLOCK, K_BLOCK)
            z = scores(q_stack[r], k_ref[pl.ds(start, K_BLOCK), :])
            sums = tile_sums(z)
            acc = acc + jnp.dot(tile_weights(z, run, sums), v_ref[pl.ds(start, K_BLOCK), :],
                                preferred_element_type=jnp.float32)
            run = run + sums[:, K_BLOCK:]
            return t + 1, (jnp.min(run) > STICK_GONE).astype(jnp.int32), run, acc

        _, _, _, acc = lax.while_loop(cond, body, (jnp.int32(3), jnp.int32(0), runs[r], accs[r]))
        write_out(r, acc)

    for r in range(n_sub):
        write_out(r, accs[r])
    least = runs[0]
    for r in range(1, n_sub):
        least = jnp.minimum(least, runs[r])

    @pl.when(jnp.min(least) <= STICK_GONE)
    def _():
        for r in range(n_sub):
            pl.when(jnp.min(runs[r]) <= STICK_GONE)(lambda r=r: walk_on(r))


def _scan_matrices():
    j = lax.broadcasted_iota(jnp.int32, (2 * K_BLOCK, 2 * K_BLOCK), 0)
    s = lax.broadcasted_iota(jnp.int32, (2 * K_BLOCK, 2 * K_BLOCK), 1)
    tri = (j >= s).astype(jnp.bfloat16)
    scan = ((s >= K_BLOCK) | (j % K_BLOCK >= s)).astype(jnp.bfloat16)
    return tri, scan


def _attention(q, kv, batch, l_pad):
    q4 = q.reshape(N_HEAD_TILES, batch, l_pad, LANES)
    kv4 = kv.reshape(2 * N_HEAD_TILES, batch, l_pad, LANES)
    tri, scan = _scan_matrices()
    q_spec = pl.BlockSpec((None, None, Q_BLOCK, LANES), lambda b, p, i: (p, b, i, 0))
    out = pl.pallas_call(
        _attention_kernel,
        out_shape=jax.ShapeDtypeStruct(q4.shape, jnp.bfloat16),
        grid=(batch, N_HEAD_TILES, l_pad // Q_BLOCK),
        in_specs=[q_spec,
                  pl.BlockSpec((None, None, l_pad, LANES), lambda b, p, i: (p, b, 0, 0)),
                  pl.BlockSpec((None, None, l_pad, LANES),
                               lambda b, p, i: (N_HEAD_TILES + p, b, 0, 0)),
                  _resident(tri.shape), _resident(scan.shape)],
        out_specs=q_spec,
        compiler_params=_params(("parallel", "parallel", "arbitrary")),
        name="stick_breaking_attention",
    )(q4, kv4, kv4, tri, scan)
    return out.reshape(N_HEAD_TILES, batch * l_pad, LANES)


def kernel(x, meta_tokens, norm_gains, conv_in_proj, conv_w, conv_out_proj, kv_norm,
           w_k, w_v, w_q, w_o, mlp_w1, mlp_w2):
    batch, seq, _ = x.shape
    depth = norm_gains.shape[0]
    n_conv = conv_in_proj.shape[0]
    first = FRONT_PAD
    last = first + N_META + seq
    l_pad = -(-(last + CONV_WIDTH - 1) // Q_BLOCK) * Q_BLOCK
    while (batch * l_pad) % ROW_BLOCK:
        l_pad += Q_BLOCK
    bf16 = jnp.bfloat16

    meta = jnp.broadcast_to(meta_tokens[None].astype(x.dtype), (batch, N_META, D_MODEL))
    h = jnp.concatenate([jnp.zeros((batch, first, D_MODEL), x.dtype), meta, x,
                         jnp.zeros((batch, l_pad - last, D_MODEL), x.dtype)], axis=1)
    h = h.reshape(batch * l_pad, D_MODEL)

    kv = None
    for layer in range(depth):
        g = norm_gains[layer]
        w1, w2 = mlp_w1[layer].astype(bf16), mlp_w2[layer].astype(bf16)
        if layer < n_conv:
            h = _conv_mixer(h, g, conv_in_proj[layer].astype(bf16), conv_w[layer],
                            conv_out_proj[layer].astype(bf16), w1, w2)
        else:
            if layer == n_conv:
                w_kv = jnp.concatenate([w_k, w_v], axis=1).astype(bf16)
                kv = _norm_proj(h, kv_norm[None, :], w_kv)
            j = layer - n_conv
            w_q_scaled = (w_q[j] * HEAD_DIM ** -0.5).astype(bf16)
            q = _norm_proj(h, g[0:1], w_q_scaled)
            o = _attention(q, kv, batch, l_pad)
            if layer == depth - 1 and seq % ROW_BLOCK == 0:
                out = _attn_mlp_real_rows(h, o, g, w_o[j].astype(bf16), w1, w2,
                                          batch, l_pad, first + N_META, seq)
                return out.reshape(batch, seq, D_MODEL)
            h = _attn_mlp(h, o, g, w_o[j].astype(bf16), w1, w2)
    return h.reshape(batch, l_pad, D_MODEL)[:, first + N_META:last]
```
